```python
import math
import jax, jax.numpy as jnp
from jax import lax
import numpy as np

D_MODEL = 1024
BATCH = 2
SEQ = 8192
DEPTH = 4

N_MIXERS = 2
N_RET_LAYERS = (DEPTH + 1) // 2
N_DIFF_LAYERS = DEPTH // 2
RMS_EPS = 1e-6

RET_HEADS = 4
RET_DK = D_MODEL // RET_HEADS
RET_DV = 2 * RET_DK
RET_CHUNK = 128
ROPE_BASE = 10000.0
RET_IN = 2 * RET_HEADS * RET_DK + 2 * RET_HEADS * RET_DV

DIFF_HEADS = 8
DIFF_DH = D_MODEL // (2 * DIFF_HEADS)
DIFF_DV = 2 * DIFF_DH
DIFF_IN = 4 * DIFF_HEADS * DIFF_DH + DIFF_HEADS * DIFF_DV
Q_BLOCK = 128
NEG_BIG = -1e30

REL_BUCKETS = 32
REL_MAX_EXACT = REL_BUCKETS // 2
REL_MAX_DIST = 128

D_FF = -(-8 * D_MODEL // (3 * 256)) * 256

kernel_name = "hybrid_retention_diffattn_sandwich"


def rmsnorm(x, g):
    xf = x.astype(jnp.float32)
    y = xf * lax.rsqrt(jnp.mean(xf * xf, axis=-1, keepdims=True) + RMS_EPS)
    return (y * g.astype(jnp.float32)).astype(x.dtype)


def rms_unit(xf):
    return xf * lax.rsqrt(jnp.mean(xf * xf, axis=-1, keepdims=True) + RMS_EPS)


def rotary(x, pos):
    half = x.shape[-1] // 2
    inv = ROPE_BASE ** (-jnp.arange(half, dtype=jnp.float32) / half)
    ang = pos[:, None] * inv[None, :]
    cos, sin = jnp.cos(ang), jnp.sin(ang)
    x1, x2 = x[..., :half], x[..., half:]
    return jnp.concatenate([x1 * cos - x2 * sin, x1 * sin + x2 * cos], axis=-1)


def retention_mixer(h, w_in, w_out):
    B, S, _ = h.shape
    H, C = RET_HEADS, RET_CHUNK
    n_chunks = S // C
    proj = h @ w_in
    q, k, v, gate = jnp.split(
        proj, [H * RET_DK, 2 * H * RET_DK, 2 * H * RET_DK + H * RET_DV], axis=-1)
    pos = jnp.arange(S, dtype=jnp.float32)
    q = q.astype(jnp.float32).reshape(B, S, H, RET_DK).transpose(0, 2, 1, 3)
    k = k.astype(jnp.float32).reshape(B, S, H, RET_DK).transpose(0, 2, 1, 3)
    v = v.astype(jnp.float32).reshape(B, S, H, RET_DV).transpose(0, 2, 1, 3)
    q = rotary(q, pos)
    k = rotary(k, pos) * (RET_DK ** -0.5)

    log_gamma = jnp.log1p(-(2.0 ** (-5.0 - jnp.arange(H, dtype=jnp.float32))))
    idx = jnp.arange(C, dtype=jnp.float32)
    diff = idx[:, None] - idx[None, :]
    decay_mask = jnp.where(diff[None] >= 0,
                           jnp.exp(jnp.maximum(diff, 0.0)[None] * log_gamma[:, None, None]),
                           0.0)
    q_decay = jnp.exp((idx[None, :] + 1.0) * log_gamma[:, None])[None, :, :, None]
    k_decay = jnp.exp((C - 1.0 - idx[None, :]) * log_gamma[:, None])[None, :, :, None]
    chunk_decay = jnp.exp(C * log_gamma)[None, :, None, None]

    def to_chunks(t):
        return t.reshape(B, H, n_chunks, C, t.shape[-1]).transpose(2, 0, 1, 3, 4)

    def step(state, inp):
        qc, kc, vc = inp
        inner = jnp.einsum('bhnd,bhmd->bhnm', qc, kc) * decay_mask[None]
        out = jnp.einsum('bhnm,bhme->bhne', inner, vc)
        out = out + jnp.einsum('bhnd,bhde->bhne', qc, state) * q_decay
        state = state * chunk_decay + jnp.einsum('bhmd,bhme->bhde', kc * k_decay, vc)
        return state, out

    state0 = jnp.zeros((B, H, RET_DK, RET_DV), jnp.float32)
    _, ys = lax.scan(step, state0, (to_chunks(q), to_chunks(k), to_chunks(v)))
    y = ys.transpose(1, 0, 3, 2, 4).reshape(B, S, H, RET_DV)
    y = rms_unit(y).reshape(B, S, H * RET_DV)
    y = y * jax.nn.silu(gate.astype(jnp.float32))
    return y.astype(h.dtype) @ w_out


def t5_causal_bucket(rel):
    n = jnp.maximum(rel, 0)
    nf = jnp.maximum(n, 1).astype(jnp.float32)
    large = REL_MAX_EXACT + (jnp.log(nf / REL_MAX_EXACT)
                             / math.log(REL_MAX_DIST / REL_MAX_EXACT)
                             * (REL_BUCKETS - REL_MAX_EXACT)).astype(jnp.int32)
    large = jnp.minimum(large, REL_BUCKETS - 1)
    return jnp.where(n < REL_MAX_EXACT, n, large)


def diff_attention_mixer(h, w_in, w_out, lam, subln_g, rel_table, lambda_init):
    B, S, _ = h.shape
    H, d = DIFF_HEADS, DIFF_DH
    nb = S // Q_BLOCK
    proj = h @ w_in
    q, k, v = jnp.split(proj, [2 * H * d, 4 * H * d], axis=-1)
    q = q.reshape(B, S, H, 2, d).transpose(0, 2, 3, 1, 4) * (d ** -0.5)
    k = k.reshape(B, S, H, 2, d).transpose(0, 2, 3, 1, 4)
    v = v.reshape(B, S, H, DIFF_DV).transpose(0, 2, 1, 3)
    lamf = lam.astype(jnp.float32)
    lam_full = (jnp.exp(jnp.sum(lamf[0] * lamf[1])) - jnp.exp(jnp.sum(lamf[2] * lamf[3]))
                + lambda_init)
    q_blocks = q.reshape(B, H, 2, nb, Q_BLOCK, d).transpose(3, 0, 1, 2, 4, 5)
    k_pos = jnp.arange(S, dtype=jnp.int32)

    def block(args):
        qb, bi = args
        q_pos = bi * Q_BLOCK + jnp.arange(Q_BLOCK, dtype=jnp.int32)
        rel = q_pos[:, None] - k_pos[None, :]
        bias = rel_table[t5_causal_bucket(rel)].astype(jnp.float32).transpose(2, 0, 1)
        logits = jnp.einsum('bhiqd,bhikd->bhiqk', qb, k).astype(jnp.float32)
        logits = logits + bias[None, :, None]
        logits = jnp.where(rel[None, None, None] >= 0, logits, NEG_BIG)
        p = jax.nn.softmax(logits, axis=-1)
        attn = p[:, :, 0] - lam_full * p[:, :, 1]
        return jnp.einsum('bhqk,bhkd->bhqd', attn.astype(v.dtype), v)

    o = lax.map(block, (q_blocks, jnp.arange(nb, dtype=jnp.int32)))
    o = o.transpose(1, 2, 0, 3, 4).reshape(B, H, S, DIFF_DV)
    o = rmsnorm(o, subln_g) * (1.0 - lambda_init)
    o = o.transpose(0, 2, 1, 3).reshape(B, S, H * DIFF_DV)
    return o @ w_out


def swiglu(h, w_gate, w_up, w_down):
    return (jax.nn.silu(h @ w_gate) * (h @ w_up)) @ w_down


def setup_inputs(seed: int = 0) -> dict:
    key = jax.random.key(seed)
    ks = jax.random.split(key, 12)
    f32 = jnp.float32
    nrm = jax.random.normal
    return {
        "x": nrm(ks[0], (BATCH, SEQ, D_MODEL), f32),
        "norm_gains": 1.0 + 0.05 * nrm(ks[1], (DEPTH, 4, D_MODEL), f32),
        "ret_w_in": nrm(ks[2], (N_RET_LAYERS, D_MODEL, RET_IN), f32) * D_MODEL ** -0.5,
        "ret_w_out": nrm(ks[3], (N_RET_LAYERS, RET_HEADS * RET_DV, D_MODEL), f32)
                     * (RET_HEADS * RET_DV) ** -0.5,
        "diff_w_in": nrm(ks[4], (N_DIFF_LAYERS, D_MODEL, DIFF_IN), f32) * D_MODEL ** -0.5,
        "diff_w_out": nrm(ks[5], (N_DIFF_LAYERS, DIFF_HEADS * DIFF_DV, D_MODEL), f32)
                      * (DIFF_HEADS * DIFF_DV) ** -0.5,
        "diff_lambda": 0.1 * nrm(ks[6], (N_DIFF_LAYERS, 4, DIFF_DH), f32),
        "diff_subln": 1.0 + 0.05 * nrm(ks[7], (N_DIFF_LAYERS, DIFF_DV), f32),
        "rel_bias_table": 0.5 * nrm(ks[8], (REL_BUCKETS, DIFF_HEADS), f32),
        "ffn_w_gate": nrm(ks[9], (DEPTH, D_MODEL, D_FF), f32) * D_MODEL ** -0.5,
        "ffn_w_up": nrm(ks[10], (DEPTH, D_MODEL, D_FF), f32) * D_MODEL ** -0.5,
        "ffn_w_down": nrm(ks[11], (DEPTH, D_FF, D_MODEL), f32) * D_FF ** -0.5,
    }


def reference(x, norm_gains, ret_w_in, ret_w_out, diff_w_in, diff_w_out, diff_lambda,
              diff_subln, rel_bias_table, ffn_w_gate, ffn_w_up, ffn_w_down):
    for i in range(DEPTH):
        g = norm_gains[i]
        h = rmsnorm(x, g[0])
        j = i // N_MIXERS
        if i % N_MIXERS == 0:
            m = retention_mixer(h, ret_w_in[j], ret_w_out[j])
        else:
            lambda_init = 0.8 - 0.6 * math.exp(-0.3 * i)
            m = diff_attention_mixer(h, diff_w_in[j], diff_w_out[j], diff_lambda[j],
                                     diff_subln[j], rel_bias_table, lambda_init)
        x = x + rmsnorm(m, g[1])
        h = rmsnorm(x, g[2])
        x = x + rmsnorm(swiglu(h, ffn_w_gate[i], ffn_w_up[i], ffn_w_down[i]), g[3])
    return x
```

```python
import functools
import math

import numpy as np
import jax
import jax.numpy as jnp
from jax import lax
from jax.experimental import pallas as pl
from jax.experimental.pallas import tpu as pltpu

D_MODEL = 1024
DEPTH = 4
N_MIXERS = 2
RMS_EPS = 1e-6

RET_HEADS = 4
RET_DK = D_MODEL // RET_HEADS
RET_DV = 2 * RET_DK
RET_CHUNK = 128
ROPE_BASE = 10000.0

DIFF_HEADS = 8
DIFF_DH = D_MODEL // (2 * DIFF_HEADS)
DIFF_DV = 2 * DIFF_DH
NEG_BIG = -1e30

REL_BUCKETS = 32
REL_MAX_EXACT = REL_BUCKETS // 2
REL_MAX_DIST = 128

D_FF = -(-8 * D_MODEL // (3 * 256)) * 256

VMEM_LIMIT_BYTES = 56 * 1024 * 1024

ROW_TILE = 1024
IN_PROJ_COL_TILE = 512
FFN_COL_TILE = 256
RET_SEQ_TILE = 1024
ATTN_TILE = 512

F32 = jnp.float32
BF16 = jnp.bfloat16


def _params(*semantics):
    return pltpu.CompilerParams(dimension_semantics=semantics,
                                vmem_limit_bytes=VMEM_LIMIT_BYTES)


def _rms_unit(xf):
    return xf * lax.rsqrt(jnp.mean(xf * xf, axis=-1, keepdims=True) + RMS_EPS)


def _norm_matmul_kernel(x_ref, g_ref, w_ref, o_ref, h_ref):
    @pl.when(pl.program_id(1) == 0)
    def _():
        h_ref[...] = (_rms_unit(x_ref[...]) * g_ref[...]).astype(BF16)

    o_ref[...] = jnp.dot(h_ref[...], w_ref[...],
                         preferred_element_type=F32).astype(o_ref.dtype)


def _norm_matmul(x2d, g, w):
    t, d = x2d.shape
    n = w.shape[1]
    tm, tn = ROW_TILE, IN_PROJ_COL_TILE
    return pl.pallas_call(
        _norm_matmul_kernel,
        grid=(t // tm, n // tn),
        in_specs=[pl.BlockSpec((tm, d), lambda i, j: (i, 0)),
                  pl.BlockSpec((1, d), lambda i, j: (0, 0)),
                  pl.BlockSpec((d, tn), lambda i, j: (0, j))],
        out_specs=pl.BlockSpec((tm, tn), lambda i, j: (i, j)),
        out_shape=jax.ShapeDtypeStruct((t, n), BF16),
        scratch_shapes=[pltpu.VMEM((tm, d), BF16)],
        compiler_params=_params("parallel", "arbitrary"),
        name="norm_matmul",
    )(x2d, g, w)


def _norm_matmul_t_kernel(x_ref, g_ref, wt_ref, o_ref):
    h = (_rms_unit(x_ref[...]) * g_ref[...]).astype(BF16)
    o_ref[...] = lax.dot_general(wt_ref[...], h, (((1,), (1,)), ((), ())),
                                 preferred_element_type=F32).astype(o_ref.dtype)


def _norm_matmul_t(x2d, g, wt, tk):
    t, d = x2d.shape
    n = wt.shape[0]
    return pl.pallas_call(
        _norm_matmul_t_kernel,
        grid=(t // tk,),
        in_specs=[pl.BlockSpec((tk, d), lambda i: (i, 0)),
                  pl.BlockSpec((1, d), lambda i: (0, 0)),
                  pl.BlockSpec((n, d), lambda i: (0, 0))],
        out_specs=pl.BlockSpec((None, n, tk), lambda i: (i, 0, 0)),
        out_shape=jax.ShapeDtypeStruct((t // tk, n, tk), BF16),
        compiler_params=_params("parallel"),
        name="norm_matmul_t",
    )(x2d, g, wt)


def _out_proj_kernel(y_ref, w_ref, x_ref, g_ref, o_ref):
    m = jnp.dot(y_ref[...], w_ref[...], preferred_element_type=F32)
    o_ref[...] = x_ref[...] + _rms_unit(m) * g_ref[...]


def _out_proj(y2d, w, x2d, g):
    t, k = y2d.shape
    d = w.shape[1]
    tm = ROW_TILE // 2
    return pl.pallas_call(
        _out_proj_kernel,
        grid=(t // tm,),
        in_specs=[pl.BlockSpec((tm, k), lambda i: (i, 0)),
                  pl.BlockSpec((k, d), lambda i: (0, 0)),
                  pl.BlockSpec((tm, d), lambda i: (i, 0)),
                  pl.BlockSpec((1, d), lambda i: (0, 0))],
        out_specs=pl.BlockSpec((tm, d), lambda i: (i, 0)),
        out_shape=jax.ShapeDtypeStruct((t, d), F32),
        compiler_params=_params("parallel"),
        name="out_proj",
    )(y2d, w, x2d, g)


def _ffn_kernel(x_ref, gpre_ref, wg_ref, wu_ref, wd_ref, gpost_ref, o_ref, h_ref, acc_ref):
    f = pl.program_id(1)

    @pl.when(f == 0)
    def _():
        h_ref[...] = (_rms_unit(x_ref[...]) * gpre_ref[...]).astype(BF16)
        acc_ref[...] = jnp.zeros_like(acc_ref)

    h = h_ref[...]
    gate = jnp.dot(h, wg_ref[...], preferred_element_type=F32)
    up = jnp.dot(h, wu_ref[...], preferred_element_type=F32)
    a = (gate * jax.nn.sigmoid(gate) * up).astype(BF16)
    acc_ref[...] += jnp.dot(a, wd_ref[...], preferred_element_type=F32)

    @pl.when(f == pl.num_programs(1) - 1)
    def _():
        o_ref[...] = x_ref[...] + _rms_unit(acc_ref[...]) * gpost_ref[...]


def _ffn(x2d, g_pre, w_gate, w_up, w_down, g_post):
    t, d = x2d.shape
    ff = w_gate.shape[1]
    tm, tf = ROW_TILE, FFN_COL_TILE
    return pl.pallas_call(
        _ffn_kernel,
        grid=(t // tm, ff // tf),
        in_specs=[pl.BlockSpec((tm, d), lambda i, f: (i, 0)),
                  pl.BlockSpec((1, d), lambda i, f: (0, 0)),
                  pl.BlockSpec((d, tf), lambda i, f: (0, f)),
                  pl.BlockSpec((d, tf), lambda i, f: (0, f)),
                  pl.BlockSpec((tf, d), lambda i, f: (f, 0)),
                  pl.BlockSpec((1, d), lambda i, f: (0, 0))],
        out_specs=pl.BlockSpec((tm, d), lambda i, f: (i, 0)),
        out_shape=jax.ShapeDtypeStruct((t, d), F32),
        scratch_shapes=[pltpu.VMEM((tm, d), BF16), pltpu.VMEM((tm, d), F32)],
        compiler_params=_params("parallel", "arbitrary"),
        name="ffn",
    )(x2d, g_pre, w_gate, w_up, w_down, g_post)


def _retention_kernel(lg_ref, q_ref, k_ref, v_ref, gate_ref, cos_ref, sin_ref, o_ref,
                      state_ref, *, n_chunks):
    c_len = RET_CHUNK
    half = RET_DK // 2

    @pl.when(pl.program_id(2) == 0)
    def _():
        state_ref[...] = jnp.zeros_like(state_ref)

    lg = lg_ref[pl.program_id(1)]
    ri = lax.broadcasted_iota(jnp.int32, (c_len, c_len), 0)
    ci = lax.broadcasted_iota(jnp.int32, (c_len, c_len), 1)
    diff = (ri - ci).astype(F32)
    decay_mask = jnp.where(diff >= 0, jnp.exp(jnp.maximum(diff, 0.0) * lg), 0.0)
    idx = lax.broadcasted_iota(jnp.int32, (c_len, 1), 0).astype(F32)
    q_decay = jnp.exp((idx + 1.0) * lg)
    k_decay = jnp.exp((c_len - 1.0 - idx) * lg)
    chunk_decay = jnp.exp(jnp.full((1, 1), c_len, F32) * lg)

    def rotate(t, cos, sin):
        t1, t2 = t[:, :half], t[:, half:]
        return jnp.concatenate([t1 * cos - t2 * sin, t1 * sin + t2 * cos], axis=1)

    def chunk(c, carry):
        rows = pl.ds(pl.multiple_of(c * c_len, c_len), c_len)
        cos = cos_ref[rows, :]
        sin = sin_ref[rows, :]
        q = rotate(q_ref[rows, :].astype(F32), cos, sin)
        k = rotate(k_ref[rows, :].astype(F32), cos, sin) * (RET_DK ** -0.5)
        qb = q.astype(BF16)
        kb = k.astype(BF16)
        v = v_ref[rows, :]
        inner = lax.dot_general(qb, kb, (((1,), (1,)), ((), ())),
                                preferred_element_type=F32) * decay_mask
        out = jnp.dot(inner.astype(BF16), v, preferred_element_type=F32)
        state = state_ref[...]
        out = out + jnp.dot(qb, state.astype(BF16), preferred_element_type=F32) * q_decay
        kd_t = jnp.transpose(k * k_decay).astype(BF16)
        state_ref[...] = state * chunk_decay + jnp.dot(kd_t, v, preferred_element_type=F32)
        g = gate_ref[rows, :].astype(F32)
        o_ref[rows, :] = (_rms_unit(out) * (g * jax.nn.sigmoid(g))).astype(o_ref.dtype)
        return carry

    lax.fori_loop(0, n_chunks, chunk, 0)


def _retention(proj, log_gamma, cos, sin):
    b, s, _ = proj.shape
    ts = RET_SEQ_TILE
    qk_off = RET_HEADS
    v_off = 2 * RET_HEADS * RET_DK // RET_DV
    g_off = v_off + RET_HEADS
    return pl.pallas_call(
        functools.partial(_retention_kernel, n_chunks=ts // RET_CHUNK),
        grid=(b, RET_HEADS, s // ts),
        in_specs=[pl.BlockSpec(memory_space=pltpu.SMEM),
                  pl.BlockSpec((None, ts, RET_DK), lambda bi, h, si: (bi, si, h)),
                  pl.BlockSpec((None, ts, RET_DK), lambda bi, h, si: (bi, si, qk_off + h)),
                  pl.BlockSpec((None, ts, RET_DV), lambda bi, h, si: (bi, si, v_off + h)),
                  pl.BlockSpec((None, ts, RET_DV), lambda bi, h, si: (bi, si, g_off + h)),
                  pl.BlockSpec((ts, RET_DK // 2), lambda bi, h, si: (si, 0)),
                  pl.BlockSpec((ts, RET_DK // 2), lambda bi, h, si: (si, 0))],
        out_specs=pl.BlockSpec((None, ts, RET_DV), lambda bi, h, si: (bi, si, h)),
        out_shape=jax.ShapeDtypeStruct((b, s, RET_HEADS * RET_DV), BF16),
        scratch_shapes=[pltpu.VMEM((RET_DK, RET_DV), F32)],
        compiler_params=_params("parallel", "parallel", "arbitrary"),
        name="retention",
    )(log_gamma, proj, proj, proj, proj, cos, sin)


def _bucket_upper_bounds():
    n = np.arange(0, 4 * REL_MAX_DIST, dtype=np.int64)
    nf = np.maximum(n, 1).astype(np.float32)
    ratio = np.log(nf / np.float32(REL_MAX_EXACT)) / np.float32(
        math.log(REL_MAX_DIST / REL_MAX_EXACT)) * np.float32(REL_BUCKETS - REL_MAX_EXACT)
    large = np.minimum(REL_MAX_EXACT + ratio.astype(np.int32), REL_BUCKETS - 1)
    bucket = np.where(n < REL_MAX_EXACT, n, large)
    assert np.all(np.diff(bucket) >= 0) and bucket[REL_MAX_DIST] == REL_BUCKETS - 1
    return [int(n[bucket == b].max()) for b in range(REL_BUCKETS - 1)]


def _bias_tiles_kernel(tab_ref, o_ref, *, tile, upper):
    h = pl.program_id(0)
    key = lax.broadcasted_iota(jnp.int32, (tile, tile), 0)
    qry = lax.broadcasted_iota(jnp.int32, (tile, tile), 1)
    far = tab_ref[h, REL_BUCKETS - 1]
    for t in range(2):
        dist = qry - key + t * tile
        val = jnp.zeros((tile, tile), F32)
        for b in range(REL_BUCKETS - 2, -1, -1):
            val = jnp.where(dist <= upper[b], tab_ref[h, b] - far, val)
        o_ref[t] = val


def _bias_tiles(rel_table_t, tile):
    assert tile >= REL_MAX_DIST
    return pl.pallas_call(
        functools.partial(_bias_tiles_kernel, tile=tile, upper=_bucket_upper_bounds()),
        grid=(DIFF_HEADS,),
        in_specs=[pl.BlockSpec(memory_space=pltpu.SMEM)],
        out_specs=pl.BlockSpec((None, 2, tile, tile), lambda h: (h, 0, 0, 0)),
        out_shape=jax.ShapeDtypeStruct((DIFF_HEADS, 2, tile, tile), F32),
        compiler_params=_params("parallel"),
        name="bias_tiles",
    )(rel_table_t)


def _diff_attn_kernel(q_ref, k_ref, vt_ref, bias_ref, lam_ref, g_ref, o_ref,
                      qs_ref, m_ref, l_ref, acc_ref, *, tile, lambda_init):
    i = pl.program_id(2)
    r = 2 * tile

    q_t = jnp.transpose(q_ref[...].astype(F32) * (DIFF_DH ** -0.5))
    feat = lax.broadcasted_iota(jnp.int32, q_t.shape, 0)
    qs_ref[:, :tile] = jnp.where(feat < DIFF_DH, q_t, 0.0).astype(BF16)
    qs_ref[:, tile:] = jnp.where(feat >= DIFF_DH, q_t, 0.0).astype(BF16)
    m_ref[...] = jnp.full_like(m_ref, NEG_BIG)
    l_ref[...] = jnp.zeros_like(l_ref)
    acc_ref[...] = jnp.zeros_like(acc_ref)

    def step(j, bias_t, diagonal):
        kj = k_ref[pl.ds(pl.multiple_of(j * tile, tile), tile), :]
        s = jnp.dot(kj, qs_ref[...], preferred_element_type=F32)
        if bias_t is not None:
            s = s + jnp.concatenate([bias_t, bias_t], axis=1)
        if diagonal:
            key = lax.broadcasted_iota(jnp.int32, (tile, r), 0)
            col = lax.broadcasted_iota(jnp.int32, (tile, r), 1)
            qry = jnp.where(col >= tile, col - tile, col)
            s = jnp.where(key <= qry, s, NEG_BIG)
        m_prev = m_ref[...]
        m_new = jnp.maximum(m_prev, jnp.max(s, axis=0, keepdims=True))
        alpha = jnp.exp(m_prev - m_new)
        p = jnp.exp(s - m_new)
        l_ref[...] = alpha * l_ref[...] + jnp.sum(p, axis=0, keepdims=True)
        acc_ref[...] = alpha * acc_ref[...] + jnp.dot(
            vt_ref[j], p.astype(BF16), preferred_element_type=F32)
        m_ref[...] = m_new

    def far_step(j, carry):
        step(j, None, False)
        return carry

    lax.fori_loop(0, jnp.maximum(i - 1, 0), far_step, 0)

    @pl.when(i >= 1)
    def _():
        step(i - 1, bias_ref[1], False)

    step(i, bias_ref[0], True)

    lam = lam_ref[...]
    lam_full = (jnp.exp(jnp.sum(lam[0:1] * lam[1:2], axis=1, keepdims=True))
                - jnp.exp(jnp.sum(lam[2:3] * lam[3:4], axis=1, keepdims=True))
                + lambda_init)
    a = acc_ref[...] * (1.0 / l_ref[...])
    o = jnp.transpose(a[:, :tile] - lam_full * a[:, tile:])
    o_ref[...] = (_rms_unit(o) * g_ref[...] * (1.0 - lambda_init)).astype(o_ref.dtype)


def _diff_attention(qk, vt, bias, lam, subln, lambda_init):
    b, s, _ = qk.shape
    tile = ATTN_TILE
    nk = s // tile
    k_off = 2 * DIFF_HEADS * DIFF_DH // (2 * DIFF_DH)
    return pl.pallas_call(
        functools.partial(_diff_attn_kernel, tile=tile, lambda_init=lambda_init),
        grid=(b, DIFF_HEADS, nk),
        in_specs=[pl.BlockSpec((None, tile, 2 * DIFF_DH), lambda bi, h, i: (bi, i, h)),
                  pl.BlockSpec((None, s, 2 * DIFF_DH), lambda bi, h, i: (bi, 0, k_off + h)),
                  pl.BlockSpec((None, nk, DIFF_DV, tile), lambda bi, h, i: (bi, 0, h, 0)),
                  pl.BlockSpec((None, 2, tile, tile), lambda bi, h, i: (h, 0, 0, 0)),
                  pl.BlockSpec((4, DIFF_DH), lambda bi, h, i: (0, 0)),
                  pl.BlockSpec((1, DIFF_DV), lambda bi, h, i: (0, 0))],
        out_specs=pl.BlockSpec((None, tile, DIFF_DV), lambda bi, h, i: (bi, i, h)),
        out_shape=jax.ShapeDtypeStruct((b, s, DIFF_HEADS * DIFF_DV), BF16),
        scratch_shapes=[pltpu.VMEM((2 * DIFF_DH, 2 * tile), BF16),
                        pltpu.VMEM((1, 2 * tile), F32),
                        pltpu.VMEM((1, 2 * tile), F32),
                        pltpu.VMEM((DIFF_DV, 2 * tile), F32)],
        compiler_params=_params("parallel", "parallel", "arbitrary"),
        name="diff_attention",
    )(qk, qk, vt, bias, lam, subln)


def kernel(x, norm_gains, ret_w_in, ret_w_out, diff_w_in, diff_w_out, diff_lambda,
           diff_subln, rel_bias_table, ffn_w_gate, ffn_w_up, ffn_w_down):
    b, s, d = x.shape
    t = b * s
    x2d = x.reshape(t, d)
    gains = norm_gains.reshape(DEPTH, 4, 1, d)

    half = RET_DK // 2
    inv = ROPE_BASE ** (-jnp.arange(half, dtype=F32) / half)
    ang = jnp.arange(s, dtype=F32)[:, None] * inv[None, :]
    cos, sin = jnp.cos(ang), jnp.sin(ang)
    log_gamma = jnp.log1p(-(2.0 ** (-5.0 - jnp.arange(RET_HEADS, dtype=F32))))

    bias = _bias_tiles(rel_bias_table.T, ATTN_TILE)
    n_qk = 4 * DIFF_HEADS * DIFF_DH

    for i in range(DEPTH):
        g = gains[i]
        j = i // N_MIXERS
        if i % N_MIXERS == 0:
            proj = _norm_matmul(x2d, g[0], ret_w_in[j].astype(BF16))
            y = _retention(proj.reshape(b, s, -1), log_gamma, cos, sin)
            w_out = ret_w_out[j]
        else:
            lambda_init = 0.8 - 0.6 * math.exp(-0.3 * i)
            w_in = diff_w_in[j].astype(BF16)
            qk = _norm_matmul(x2d, g[0], w_in[:, :n_qk])
            vt = _norm_matmul_t(x2d, g[0], w_in[:, n_qk:].T, ATTN_TILE)
            y = _diff_attention(qk.reshape(b, s, n_qk),
                                vt.reshape(b, s // ATTN_TILE, DIFF_HEADS * DIFF_DV, ATTN_TILE),
                                bias, diff_lambda[j], diff_subln[j].reshape(1, DIFF_DV),
                                lambda_init)
            w_out = diff_w_out[j]
        x2d = _out_proj(y.reshape(t, -1), w_out.astype(BF16), x2d, g[1])
        x2d = _ffn(x2d, g[2], ffn_w_gate[i].astype(BF16), ffn_w_up[i].astype(BF16),
                   ffn_w_down[i].astype(BF16), g[3])
    return x2d.reshape(b, s, d)
```

```python
import functools
import math

import numpy as np
import jax
import jax.numpy as jnp
from jax import lax
from jax.experimental import pallas as pl
from jax.experimental.pallas import tpu as pltpu

D_MODEL = 1024
DEPTH = 4
N_MIXERS = 2
RMS_EPS = 1e-6

RET_HEADS = 4
RET_DK = D_MODEL // RET_HEADS
RET_DV = 2 * RET_DK
RET_IN = 2 * RET_HEADS * RET_DK + 2 * RET_HEADS * RET_DV
RET_CHUNK = 256
ROPE_BASE = 10000.0

DIFF_HEADS = 8
DIFF_DH = D_MODEL // (2 * DIFF_HEADS)
DIFF_DV = 2 * DIFF_DH
NEG_BIG = -1e30

REL_BUCKETS = 32
REL_MAX_EXACT = REL_BUCKETS // 2
REL_MAX_DIST = 128

D_FF = -(-8 * D_MODEL // (3 * 256)) * 256

VMEM_LIMIT_BYTES = 56 * 1024 * 1024

IN_PROJ_COL_TILE = 512
POST_ROW_TILE = 512
FFN_COL_TILE = 256
RET_SEQ_TILE = 2048
ATTN_TILE = 512
IN_PROJ_ROW_TILE = ATTN_TILE
ATTN_UNROLL = 8
NEAR_UNROLL = 4
BIAS_DIAGONAL, BIAS_PREVIOUS, BIAS_MASKED = 0, 1, 2
SUM_ROWS = 16
LOG2_E = math.log2(math.e)

F32 = jnp.float32
BF16 = jnp.bfloat16


def _params(*semantics):
    return pltpu.CompilerParams(dimension_semantics=semantics,
                                vmem_limit_bytes=VMEM_LIMIT_BYTES)


def _rms_unit(xf):
    return xf * lax.rsqrt(jnp.mean(xf * xf, axis=-1, keepdims=True) + RMS_EPS)


def _diff_in_proj_kernel(x_ref, g_ref, wq_ref, wk_ref, wv_ref, ok_ref, oq_ref, ov_ref, h_ref):
    tm = x_ref.shape[0]
    h_ref[...] = (_rms_unit(x_ref[...]) * g_ref[...]).astype(BF16)
    for c in range(ok_ref.shape[1] // IN_PROJ_COL_TILE):
        cols = slice(c * IN_PROJ_COL_TILE, (c + 1) * IN_PROJ_COL_TILE)
        ok_ref[:, cols] = jnp.dot(h_ref[...], wk_ref[:, cols],
                                  preferred_element_type=F32).astype(ok_ref.dtype)
    feature_major = (((0,), (1,)), ((), ()))
    ov_ref[...] = lax.dot_general(wv_ref[...], h_ref[...], feature_major,
                                  preferred_element_type=F32).astype(ov_ref.dtype)
    q_t = lax.dot_general(wq_ref[...], h_ref[...], feature_major, preferred_element_type=F32)
    q_t = q_t * (DIFF_DH ** -0.5 * LOG2_E)
    feat = lax.broadcasted_iota(jnp.int32, q_t.shape, 0) % (2 * DIFF_DH)
    oq_ref[:, :tm] = jnp.where(feat < DIFF_DH, q_t, 0.0).astype(oq_ref.dtype)
    oq_ref[:, tm:] = jnp.where(feat >= DIFF_DH, q_t, 0.0).astype(oq_ref.dtype)


def _resident_weight(layer, rows, cols, col_block=0):
    return pl.BlockSpec((None, rows, cols), lambda i: (layer, 0, col_block),
                        pipeline_mode=pl.Buffered(1))


def _diff_in_proj(x2d, g, w, layer):
    t, d = x2d.shape
    n = w.shape[2] // 3
    tm = IN_PROJ_ROW_TILE
    tiles = lambda i: (i, 0, 0)
    return pl.pallas_call(
        _diff_in_proj_kernel,
        grid=(t // tm,),
        in_specs=[pl.BlockSpec((tm, d), lambda i: (i, 0)),
                  pl.BlockSpec((1, d), lambda i: (0, 0)),
                  _resident_weight(layer, d, n, 0),
                  _resident_weight(layer, d, n, 1),
                  _resident_weight(layer, d, n, 2)],
        out_specs=[pl.BlockSpec((tm, n), lambda i: (i, 0)),
                   pl.BlockSpec((None, n, 2 * tm), tiles),
                   pl.BlockSpec((None, n, tm), tiles)],
        out_shape=[jax.ShapeDtypeStruct((t, n), BF16),
                   jax.ShapeDtypeStruct((t // tm, n, 2 * tm), BF16),
                   jax.ShapeDtypeStruct((t // tm, n, tm), BF16)],
        scratch_shapes=[pltpu.VMEM((tm, d), BF16)],
        compiler_params=_params("parallel"),
        name="diff_in_proj",
    )(x2d, g, w, w, w)


def _post_mixer_kernel(y_ref, wo_ref, x_ref, gmix_ref, gpre_ref, wg_ref, wu_ref, wd_ref,
                       gpost_ref, o_ref, h_ref, a_ref, *, y_feature_major):
    contract = (((0,), (0,)), ((), ())) if y_feature_major else (((1,), (0,)), ((), ()))
    m = lax.dot_general(y_ref[...], wo_ref[...], contract, preferred_element_type=F32)
    x1 = x_ref[...] + _rms_unit(m) * gmix_ref[...]
    o_ref[...] = x1
    h_ref[...] = (_rms_unit(x1) * gpre_ref[...]).astype(BF16)
    for c in range(a_ref.shape[1] // FFN_COL_TILE):
        cols = slice(c * FFN_COL_TILE, (c + 1) * FFN_COL_TILE)
        h = h_ref[...]
        gate = jnp.dot(h, wg_ref[:, cols], preferred_element_type=F32)
        up = jnp.dot(h, wu_ref[:, cols], preferred_element_type=F32)
        a_ref[:, cols] = (gate * jax.nn.sigmoid(gate) * up).astype(BF16)
    f = jnp.dot(a_ref[...], wd_ref[...], preferred_element_type=F32)
    o_ref[...] += _rms_unit(f) * gpost_ref[...]


def _post_mixer(y, w_out, mixer_layer, x2d, g_mix, g_pre, w_gate, w_up, w_down, layer, g_post):
    t, d = x2d.shape
    k = w_out.shape[1]
    ff = w_gate.shape[2]
    tm = POST_ROW_TILE
    row = lambda i: (i, 0)
    fixed = lambda i: (0, 0)
    feature_major = y.ndim == 4
    if feature_major:
        tiles_per_seq = y.shape[1]
        assert y.shape[2:] == (k, tm)
        y_spec = pl.BlockSpec((None, None, k, tm),
                              lambda i: (i // tiles_per_seq, i % tiles_per_seq, 0, 0))
    else:
        y_spec = pl.BlockSpec((tm, k), row)
    return pl.pallas_call(
        functools.partial(_post_mixer_kernel, y_feature_major=feature_major),
        grid=(t // tm,),
        in_specs=[y_spec,
                  _resident_weight(mixer_layer, k, d),
                  pl.BlockSpec((tm, d), row),
                  pl.BlockSpec((1, d), fixed),
                  pl.BlockSpec((1, d), fixed),
                  _resident_weight(layer, d, ff),
                  _resident_weight(layer, d, ff),
                  _resident_weight(layer, ff, d),
                  pl.BlockSpec((1, d), fixed)],
        out_specs=pl.BlockSpec((tm, d), row),
        out_shape=jax.ShapeDtypeStruct((t, d), F32),
        scratch_shapes=[pltpu.VMEM((tm, d), BF16), pltpu.VMEM((tm, ff), BF16)],
        compiler_params=_params("parallel"),
        name="post_mixer",
    )(y, w_out, x2d, g_mix, g_pre, w_gate, w_up, w_down, g_post)


def _ret_in_proj_kernel(x_ref, g_ref, w_ref, cos_ref, sin_ref, q_ref, k_ref, v_ref, gate_ref,
                        h_ref):
    half = RET_DK // 2
    k_col, v_col, gate_col = (RET_HEADS * RET_DK, 2 * RET_HEADS * RET_DK,
                              2 * RET_HEADS * RET_DK + RET_HEADS * RET_DV)
    h_ref[...] = (_rms_unit(x_ref[...]) * g_ref[...]).astype(BF16)
    cos, sin = cos_ref[...], sin_ref[...]

    def project(col, width):
        return jnp.dot(h_ref[...], w_ref[:, col:col + width], preferred_element_type=F32)

    def rotate(t):
        t1, t2 = t[:, :half], t[:, half:]
        return jnp.concatenate([t1 * cos - t2 * sin, t1 * sin + t2 * cos], axis=1)

    for h in range(RET_HEADS):
        q_ref[h] = rotate(project(h * RET_DK, RET_DK)).astype(q_ref.dtype)
        k = rotate(project(k_col + h * RET_DK, RET_DK)) * (RET_DK ** -0.5)
        k_ref[h] = k.astype(k_ref.dtype)
        v_ref[h] = project(v_col + h * RET_DV, RET_DV).astype(v_ref.dtype)
        gate = project(gate_col + h * RET_DV, RET_DV)
        gate_ref[h] = (gate * jax.nn.sigmoid(gate)).astype(gate_ref.dtype)


def _ret_in_proj(x2d, g, w, layer, cos, sin):
    t, d = x2d.shape
    tm = IN_PROJ_ROW_TILE
    tiles_per_seq = cos.shape[0] // tm
    row = lambda i: (i, 0)
    pos = lambda i: (i % tiles_per_seq, 0)
    head_rows = lambda i: (0, i, 0)
    shapes = [(RET_HEADS, t, RET_DK)] * 2 + [(RET_HEADS, t, RET_DV)] * 2
    return pl.pallas_call(
        _ret_in_proj_kernel,
        grid=(t // tm,),
        in_specs=[pl.BlockSpec((tm, d), row),
                  pl.BlockSpec((1, d), lambda i: (0, 0)),
                  _resident_weight(layer, d, RET_IN),
                  pl.BlockSpec((tm, RET_DK // 2), pos),
                  pl.BlockSpec((tm, RET_DK // 2), pos)],
        out_specs=[pl.BlockSpec((RET_HEADS, tm, shape[2]), head_rows) for shape in shapes],
        out_shape=[jax.ShapeDtypeStruct(shape, BF16) for shape in shapes],
        scratch_shapes=[pltpu.VMEM((tm, d), BF16)],
        compiler_params=_params("parallel"),
        name="ret_in_proj",
    )(x2d, g, w, cos, sin)


def _retention_kernel(lg_ref, q_ref, k_ref, v_ref, gate_ref, o_ref, state_ref, *, n_chunks):
    c_len = RET_CHUNK

    @pl.when(pl.program_id(2) == 0)
    def _():
        state_ref[...] = jnp.zeros_like(state_ref)

    lg = lg_ref[pl.program_id(1)]
    ri = lax.broadcasted_iota(jnp.int32, (c_len, c_len), 0)
    ci = lax.broadcasted_iota(jnp.int32, (c_len, c_len), 1)
    diff = (ri - ci).astype(F32)
    decay_mask = jnp.where(diff >= 0, jnp.exp(jnp.maximum(diff, 0.0) * lg), 0.0)
    idx = lax.broadcasted_iota(jnp.int32, (c_len, 1), 0).astype(F32)
    q_decay = jnp.exp((idx + 1.0) * lg)
    k_decay = jnp.exp((c_len - 1.0 - idx) * lg)
    chunk_decay = jnp.exp(jnp.full((1, 1), c_len, F32) * lg)

    def chunk(c, carry):
        rows = pl.ds(pl.multiple_of(c * c_len, c_len), c_len)
        qb = q_ref[rows, :]
        kb = k_ref[rows, :]
        v = v_ref[rows, :]
        inner = lax.dot_general(qb, kb, (((1,), (1,)), ((), ())),
                                preferred_element_type=F32) * decay_mask
        out = jnp.dot(inner.astype(BF16), v, preferred_element_type=F32)
        state = state_ref[...]
        out = out + jnp.dot(qb, state.astype(BF16), preferred_element_type=F32) * q_decay
        kd_t = jnp.transpose(kb.astype(F32) * k_decay).astype(BF16)
        state_ref[...] = state * chunk_decay + jnp.dot(kd_t, v, preferred_element_type=F32)
        o_ref[rows, :] = (_rms_unit(out) * gate_ref[rows, :].astype(F32)).astype(o_ref.dtype)
        return carry

    lax.fori_loop(0, n_chunks, chunk, 0, unroll=True)


def _retention(q, k, v, gate, log_gamma, batch):
    t = q.shape[1]
    s = t // batch
    ts = RET_SEQ_TILE
    tiles_per_seq = s // ts
    head_rows = lambda bi, h, si: (h, bi * tiles_per_seq + si, 0)
    return pl.pallas_call(
        functools.partial(_retention_kernel, n_chunks=ts // RET_CHUNK),
        grid=(batch, RET_HEADS, tiles_per_seq),
        in_specs=[pl.BlockSpec(memory_space=pltpu.SMEM),
                  pl.BlockSpec((None, ts, RET_DK), head_rows),
                  pl.BlockSpec((None, ts, RET_DK), head_rows),
                  pl.BlockSpec((None, ts, RET_DV), head_rows),
                  pl.BlockSpec((None, ts, RET_DV), head_rows)],
        out_specs=pl.BlockSpec((ts, RET_DV), lambda bi, h, si: (bi * tiles_per_seq + si, h)),
        out_shape=jax.ShapeDtypeStruct((t, RET_HEADS * RET_DV), BF16),
        scratch_shapes=[pltpu.VMEM((RET_DK, RET_DV), F32)],
        compiler_params=_params("parallel", "parallel", "arbitrary"),
        name="retention",
    )(log_gamma, q, k, v, gate)


def _bucket_upper_bounds():
    n = np.arange(0, 4 * REL_MAX_DIST, dtype=np.int64)
    nf = np.maximum(n, 1).astype(np.float32)
    ratio = np.log(nf / np.float32(REL_MAX_EXACT)) / np.float32(
        math.log(REL_MAX_DIST / REL_MAX_EXACT)) * np.float32(REL_BUCKETS - REL_MAX_EXACT)
    large = np.minimum(REL_MAX_EXACT + ratio.astype(np.int32), REL_BUCKETS - 1)
    bucket = np.where(n < REL_MAX_EXACT, n, large)
    assert np.all(np.diff(bucket) >= 0) and bucket[REL_MAX_DIST] == REL_BUCKETS - 1
    return [int(n[bucket == b].max()) for b in range(REL_BUCKETS - 1)]


def _bias_tiles_kernel(tab_ref, o_ref, *, tile, upper):
    h = pl.program_id(0)
    key = lax.broadcasted_iota(jnp.int32, (tile, tile), 0)
    qry = lax.broadcasted_iota(jnp.int32, (tile, tile), 1)
    far = tab_ref[h, REL_BUCKETS - 1]
    for t in (BIAS_DIAGONAL, BIAS_PREVIOUS):
        dist = qry - key + t * tile
        val = jnp.zeros((tile, tile), F32)
        for b in range(REL_BUCKETS - 2, -1, -1):
            val = jnp.where(dist <= upper[b], (tab_ref[h, b] - far) * LOG2_E, val)
        o_ref[t] = jnp.where(dist >= 0, val, NEG_BIG)
    o_ref[BIAS_MASKED] = jnp.full((tile, tile), NEG_BIG, F32)


def _bias_tiles(rel_table_t, tile):
    assert tile >= REL_MAX_DIST
    return pl.pallas_call(
        functools.partial(_bias_tiles_kernel, tile=tile, upper=_bucket_upper_bounds()),
        grid=(DIFF_HEADS,),
        in_specs=[pl.BlockSpec(memory_space=pltpu.SMEM)],
        out_specs=pl.BlockSpec((None, 3, tile, tile), lambda h: (h, 0, 0, 0)),
        out_shape=jax.ShapeDtypeStruct((DIFF_HEADS, 3, tile, tile), F32),
        compiler_params=_params("parallel"),
        name="bias_tiles",
    )(rel_table_t)


def _diff_attn_kernel(qs_ref, k_ref, vt_ref, bias_ref, lam_ref, g_ref, o_ref,
                      s_ref, smax_ref, m_ref, acc_ref, *, tile, lambda_init):
    n_tiles = qs_ref.shape[0]
    assert ATTN_UNROLL % 2 == 0 and NEAR_UNROLL % 2 == 0 and n_tiles % NEAR_UNROLL == 0
    ones_rows = jnp.ones((SUM_ROWS, tile), BF16)
    last = n_tiles - 1

    def key_tile(j):
        return k_ref[pl.ds(pl.multiple_of(j * tile, tile), tile), :]

    def value_rows(j):
        return jnp.concatenate([vt_ref[j], ones_rows], axis=0)

    def put_scores(slot, s):
        s_ref[slot] = s
        smax_ref[slot] = jnp.max(s, axis=0, keepdims=True)

    def near_scores(i, base):
        i = jnp.minimum(i, last)
        qs = qs_ref[i]
        bias_t = bias_ref[BIAS_DIAGONAL]
        s = jnp.dot(key_tile(i), qs, preferred_element_type=F32)
        put_scores(base, s + jnp.concatenate([bias_t, bias_t], axis=1))
        bias_t = bias_ref[jnp.where(i >= 1, BIAS_PREVIOUS, BIAS_MASKED)]
        s = jnp.dot(key_tile(jnp.maximum(i - 1, 0)), qs, preferred_element_type=F32)
        put_scores(base + 1, s + jnp.concatenate([bias_t, bias_t], axis=1))

    def near_reduce(i, base):
        m_diag = smax_ref[base]
        p = jnp.exp2(s_ref[base] - m_diag).astype(BF16)
        acc = jnp.dot(value_rows(i), p, preferred_element_type=F32)
        m_new = jnp.maximum(m_diag, smax_ref[base + 1])
        p = jnp.exp2(s_ref[base + 1] - m_new).astype(BF16)
        acc = jnp.exp2(m_diag - m_new) * acc + jnp.dot(
            value_rows(jnp.maximum(i - 1, 0)), p, preferred_element_type=F32)
        acc_ref[i] = acc
        m_ref[i] = m_new

    def near_steps(g, carry):
        for u in range(NEAR_UNROLL):
            i = NEAR_UNROLL * g + u
            near_scores(i + 1, 2 * ((u + 1) % 2))
            near_reduce(i, 2 * (u % 2))
        return carry

    near_scores(jnp.int32(0), 0)
    lax.fori_loop(0, n_tiles // NEAR_UNROLL, near_steps, 0)

    def successor(i, j):
        wraps = j == i - 2
        return jnp.where(wraps, i + 1, i), jnp.where(wraps, 0, j + 1)

    def far_scores(i, j, slot):
        i = jnp.minimum(i, last)
        put_scores(slot, jnp.dot(key_tile(j), qs_ref[i], preferred_element_type=F32))

    def far_reduce(i, j, slot):
        m_prev = m_ref[i]
        m_new = jnp.maximum(m_prev, smax_ref[slot])
        p = jnp.exp2(s_ref[slot] - m_new).astype(BF16)
        acc_ref[i] = jnp.exp2(m_prev - m_new) * acc_ref[i] + jnp.dot(
            value_rows(j), p, preferred_element_type=F32)
        m_ref[i] = m_new

    lam = lam_ref[...]
    lam_full = (jnp.exp(jnp.sum(lam[0:1] * lam[1:2], axis=1, keepdims=True))
                - jnp.exp(jnp.sum(lam[2:3] * lam[3:4], axis=1, keepdims=True))
                + lambda_init)

    def emit(i):
        acc = acc_ref[i]
        inv_l = 1.0 / acc[DIFF_DV:DIFF_DV + 1]
        o_t = (acc[:DIFF_DV, :tile] * inv_l[:, :tile]
               - acc[:DIFF_DV, tile:] * (lam_full * inv_l[:, tile:]))
        scale = lax.rsqrt(jnp.mean(o_t * o_t, axis=0, keepdims=True) + RMS_EPS) * (1.0 - lambda_init)
        o_ref[i] = (o_t * scale * g_ref[...]).astype(o_ref.dtype)

    def far_steps(_, carry):
        i, j = carry
        emit(i - 1)
        for u in range(ATTN_UNROLL):
            i_next, j_next = successor(i, j)
            far_scores(i_next, j_next, (u + 1) % 2)
            far_reduce(i, j, u % 2)
            i, j = i_next, j_next
        return i, j

    far_pairs = [(i, j) for i in range(2, n_tiles) for j in range(i - 1)]
    n_loop = len(far_pairs) // ATTN_UNROLL * ATTN_UNROLL
    emitted = {far_pairs[t][0] - 1 for t in range(0, n_loop, ATTN_UNROLL)}
    first = jnp.int32(2), jnp.int32(0)
    far_scores(*first, 0)
    i, j = lax.fori_loop(0, n_loop // ATTN_UNROLL, far_steps, first)
    for u in range(n_loop, len(far_pairs)):
        i_next, j_next = successor(i, j)
        if u + 1 < len(far_pairs):
            far_scores(i_next, j_next, (u + 1) % 2)
        far_reduce(i, j, u % 2)
        i, j = i_next, j_next
    for i in range(n_tiles):
        if i not in emitted:
            emit(i)


def _diff_attention(k, qs_t, v_t, bias, lam, subln, lambda_init):
    b, s, _ = k.shape
    tile = ATTN_TILE
    nk = s // tile
    return pl.pallas_call(
        functools.partial(_diff_attn_kernel, tile=tile, lambda_init=lambda_init),
        grid=(b, DIFF_HEADS),
        in_specs=[pl.BlockSpec((None, nk, 2 * DIFF_DH, 2 * tile), lambda bi, h: (bi, 0, h, 0)),
                  pl.BlockSpec((None, s, 2 * DIFF_DH), lambda bi, h: (bi, 0, h)),
                  pl.BlockSpec((None, nk, DIFF_DV, tile), lambda bi, h: (bi, 0, h, 0)),
                  pl.BlockSpec((None, 3, tile, tile), lambda bi, h: (h, 0, 0, 0)),
                  pl.BlockSpec((4, DIFF_DH), lambda bi, h: (0, 0)),
                  pl.BlockSpec((DIFF_DV, 1), lambda bi, h: (0, 0))],
        out_specs=pl.BlockSpec((None, nk, DIFF_DV, tile), lambda bi, h: (bi, 0, h, 0)),
        out_shape=jax.ShapeDtypeStruct((b, nk, DIFF_HEADS * DIFF_DV, tile), BF16),
        scratch_shapes=[pltpu.VMEM((4, tile, 2 * tile), F32),
                        pltpu.VMEM((4, 1, 2 * tile), F32),
                        pltpu.VMEM((nk, 1, 2 * tile), F32),
                        pltpu.VMEM((nk, DIFF_DV + SUM_ROWS, 2 * tile), F32)],
        compiler_params=_params("parallel", "parallel"),
        name="diff_attention",
    )(qs_t, k, v_t, bias, lam, subln)


def kernel(x, norm_gains, ret_w_in, ret_w_out, diff_w_in, diff_w_out, diff_lambda,
           diff_subln, rel_bias_table, ffn_w_gate, ffn_w_up, ffn_w_down):
    b, s, d = x.shape
    t = b * s
    x2d = x.reshape(t, d)
    gains = norm_gains.reshape(DEPTH, 4, 1, d)

    half = RET_DK // 2
    inv = ROPE_BASE ** (-jnp.arange(half, dtype=F32) / half)
    ang = jnp.arange(s, dtype=F32)[:, None] * inv[None, :]
    cos, sin = jnp.cos(ang), jnp.sin(ang)
    log_gamma = jnp.log1p(-(2.0 ** (-5.0 - jnp.arange(RET_HEADS, dtype=F32))))

    bias = _bias_tiles(rel_bias_table.T, ATTN_TILE)

    ret_w_in, ret_w_out = ret_w_in.astype(BF16), ret_w_out.astype(BF16)
    diff_w_out = diff_w_out.astype(BF16)
    ffn_w_gate, ffn_w_up, ffn_w_down = (w.astype(BF16) for w in (ffn_w_gate, ffn_w_up, ffn_w_down))
    diff_w_in = diff_w_in.astype(BF16)
    assert 2 * DIFF_HEADS * DIFF_DH == DIFF_HEADS * DIFF_DV

    for i in range(DEPTH):
        g = gains[i]
        j = i // N_MIXERS
        if i % N_MIXERS == 0:
            q, k, v, gate = _ret_in_proj(x2d, g[0], ret_w_in, j, cos, sin)
            y = _retention(q, k, v, gate, log_gamma, b)
            w_out = ret_w_out
        else:
            lambda_init = 0.8 - 0.6 * math.exp(-0.3 * i)
            k, qs_t, v_t = _diff_in_proj(x2d, g[0], diff_w_in, j)
            nk = s // ATTN_TILE
            y = _diff_attention(k.reshape(b, s, -1),
                                qs_t.reshape(b, nk, -1, 2 * ATTN_TILE),
                                v_t.reshape(b, nk, -1, ATTN_TILE),
                                bias, diff_lambda[j], diff_subln[j].reshape(DIFF_DV, 1),
                                lambda_init)
            w_out = diff_w_out
        x2d = _post_mixer(y, w_out, j, x2d, g[1], g[2], ffn_w_gate, ffn_w_up, ffn_w_down, i, g[3])
    return x2d.reshape(b, s, d)
```

```python
import functools
import math

import numpy as np
import jax
import jax.numpy as jnp
from jax import lax
from jax.experimental import pallas as pl
from jax.experimental.pallas import tpu as pltpu

D_MODEL = 1024
DEPTH = 4
N_MIXERS = 2
RMS_EPS = 1e-6

RET_HEADS = 4
RET_DK = D_MODEL // RET_HEADS
RET_DV = 2 * RET_DK
RET_IN = 2 * RET_HEADS * RET_DK + 2 * RET_HEADS * RET_DV
RET_CHUNK = 256
ROPE_BASE = 10000.0

DIFF_HEADS = 8
DIFF_DH = D_MODEL // (2 * DIFF_HEADS)
DIFF_DV = 2 * DIFF_DH
NEG_BIG = -1e30

REL_BUCKETS = 32
REL_MAX_EXACT = REL_BUCKETS // 2
REL_MAX_DIST = 128

D_FF = -(-8 * D_MODEL // (3 * 256)) * 256

VMEM_LIMIT_BYTES = 56 * 1024 * 1024

IN_PROJ_COL_TILE = 512
IN_PROJ_ROW_GROUPS = 2
POST_ROW_TILE = 512
POST_ROW_GROUPS = 2
FFN_COL_TILE = 256
RET_SEQ_TILE = 2048
ATTN_TILE = 512
IN_PROJ_ROW_TILE = ATTN_TILE
ATTN_UNROLL = 8
NEAR_UNROLL = 4
BIAS_DIAGONAL, BIAS_PREVIOUS, BIAS_MASKED = 0, 1, 2
SUM_ROWS = 16
LOG2_E = math.log2(math.e)

F32 = jnp.float32
BF16 = jnp.bfloat16


def _params(*semantics):
    return pltpu.CompilerParams(dimension_semantics=semantics,
                                vmem_limit_bytes=VMEM_LIMIT_BYTES)


def _rms_unit(xf):
    return xf * lax.rsqrt(jnp.mean(xf * xf, axis=-1, keepdims=True) + RMS_EPS)


def _diff_in_proj_kernel(x_ref, g_ref, wq_ref, wk_ref, wv_ref, ok_ref, oq_ref, ov_ref, h_ref):
    tm = x_ref.shape[0]
    groups = [slice(g * tm // IN_PROJ_ROW_GROUPS, (g + 1) * tm // IN_PROJ_ROW_GROUPS)
              for g in range(IN_PROJ_ROW_GROUPS)]
    for rows in groups:
        h_ref[rows, :] = (_rms_unit(x_ref[rows, :]) * g_ref[...]).astype(BF16)
    feature_major = (((0,), (1,)), ((), ()))
    for rows in groups:
        h = h_ref[rows, :]
        for c in range(ok_ref.shape[1] // IN_PROJ_COL_TILE):
            cols = slice(c * IN_PROJ_COL_TILE, (c + 1) * IN_PROJ_COL_TILE)
            ok_ref[rows, cols] = jnp.dot(h, wk_ref[:, cols],
                                         preferred_element_type=F32).astype(ok_ref.dtype)
        ov_ref[:, rows] = lax.dot_general(wv_ref[...], h, feature_major,
                                          preferred_element_type=F32).astype(ov_ref.dtype)
        q_t = lax.dot_general(wq_ref[...], h, feature_major, preferred_element_type=F32)
        q_t = q_t * (DIFF_DH ** -0.5 * LOG2_E)
        feat = lax.broadcasted_iota(jnp.int32, q_t.shape, 0) % (2 * DIFF_DH)
        oq_ref[:, rows] = jnp.where(feat < DIFF_DH, q_t, 0.0).astype(oq_ref.dtype)
        oq_ref[:, slice(tm + rows.start, tm + rows.stop)] = jnp.where(
            feat >= DIFF_DH, q_t, 0.0).astype(oq_ref.dtype)


def _resident_weight(layer, rows, cols, col_block=0):
    return pl.BlockSpec((None, rows, cols), lambda i: (layer, 0, col_block),
                        pipeline_mode=pl.Buffered(1))


def _diff_in_proj(x2d, g, w, layer):
    t, d = x2d.shape
    n = w.shape[2] // 3
    tm = IN_PROJ_ROW_TILE
    tiles = lambda i: (i, 0, 0)
    return pl.pallas_call(
        _diff_in_proj_kernel,
        grid=(t // tm,),
        in_specs=[pl.BlockSpec((tm, d), lambda i: (i, 0)),
                  pl.BlockSpec((1, d), lambda i: (0, 0)),
                  _resident_weight(layer, d, n, 0),
                  _resident_weight(layer, d, n, 1),
                  _resident_weight(layer, d, n, 2)],
        out_specs=[pl.BlockSpec((tm, n), lambda i: (i, 0)),
                   pl.BlockSpec((None, n, 2 * tm), tiles),
                   pl.BlockSpec((None, n, tm), tiles)],
        out_shape=[jax.ShapeDtypeStruct((t, n), BF16),
                   jax.ShapeDtypeStruct((t // tm, n, 2 * tm), BF16),
                   jax.ShapeDtypeStruct((t // tm, n, tm), BF16)],
        scratch_shapes=[pltpu.VMEM((tm, d), BF16)],
        compiler_params=_params("parallel"),
        name="diff_in_proj",
    )(x2d, g, w, w, w)


def _post_mixer_kernel(y_ref, wo_ref, x_ref, gmix_ref, gpre_ref, wg_ref, wu_ref, wd_ref,
                       gpost_ref, o_ref, h_ref, a_ref, *, y_feature_major):
    tm = o_ref.shape[0]
    groups = [slice(g * tm // POST_ROW_GROUPS, (g + 1) * tm // POST_ROW_GROUPS)
              for g in range(POST_ROW_GROUPS)]
    for rows in groups:
        if y_feature_major:
            m = lax.dot_general(y_ref[:, rows], wo_ref[...], (((0,), (0,)), ((), ())),
                                preferred_element_type=F32)
        else:
            m = jnp.dot(y_ref[rows, :], wo_ref[...], preferred_element_type=F32)
        x1 = x_ref[rows, :] + _rms_unit(m) * gmix_ref[...]
        o_ref[rows, :] = x1
        h_ref[rows, :] = (_rms_unit(x1) * gpre_ref[...]).astype(BF16)
    for rows in groups:
        for c in range(a_ref.shape[1] // FFN_COL_TILE):
            cols = slice(c * FFN_COL_TILE, (c + 1) * FFN_COL_TILE)
            h = h_ref[rows, :]
            gate = jnp.dot(h, wg_ref[:, cols], preferred_element_type=F32)
            up = jnp.dot(h, wu_ref[:, cols], preferred_element_type=F32)
            a_ref[rows, cols] = (gate * jax.nn.sigmoid(gate) * up).astype(BF16)
    for rows in groups:
        f = jnp.dot(a_ref[rows, :], wd_ref[...], preferred_element_type=F32)
        o_ref[rows, :] += _rms_unit(f) * gpost_ref[...]


def _post_mixer(y, w_out, mixer_layer, x2d, g_mix, g_pre, w_gate, w_up, w_down, layer, g_post):
    t, d = x2d.shape
    k = w_out.shape[1]
    ff = w_gate.shape[2]
    tm = POST_ROW_TILE
    row = lambda i: (i, 0)
    fixed = lambda i: (0, 0)
    feature_major = y.ndim == 4
    if feature_major:
        tiles_per_seq = y.shape[1]
        assert y.shape[2:] == (k, tm)
        y_spec = pl.BlockSpec((None, None, k, tm),
                              lambda i: (i // tiles_per_seq, i % tiles_per_seq, 0, 0))
    else:
        y_spec = pl.BlockSpec((tm, k), row)
    return pl.pallas_call(
        functools.partial(_post_mixer_kernel, y_feature_major=feature_major),
        grid=(t // tm,),
        in_specs=[y_spec,
                  _resident_weight(mixer_layer, k, d),
                  pl.BlockSpec((tm, d), row),
                  pl.BlockSpec((1, d), fixed),
                  pl.BlockSpec((1, d), fixed),
                  _resident_weight(layer, d, ff),
                  _resident_weight(layer, d, ff),
                  _resident_weight(layer, ff, d),
                  pl.BlockSpec((1, d), fixed)],
        out_specs=pl.BlockSpec((tm, d), row),
        out_shape=jax.ShapeDtypeStruct((t, d), F32),
        scratch_shapes=[pltpu.VMEM((tm, d), BF16), pltpu.VMEM((tm, ff), BF16)],
        compiler_params=_params("parallel"),
        name="post_mixer",
    )(y, w_out, x2d, g_mix, g_pre, w_gate, w_up, w_down, g_post)


def _ret_in_proj_kernel(x_ref, g_ref, w_ref, cos_ref, sin_ref, q_ref, k_ref, v_ref, gate_ref,
                        h_ref):
    half = RET_DK // 2
    k_col, v_col, gate_col = (RET_HEADS * RET_DK, 2 * RET_HEADS * RET_DK,
                              2 * RET_HEADS * RET_DK + RET_HEADS * RET_DV)
    tm = x_ref.shape[0]
    groups = [slice(g * tm // IN_PROJ_ROW_GROUPS, (g + 1) * tm // IN_PROJ_ROW_GROUPS)
              for g in range(IN_PROJ_ROW_GROUPS)]
    for rows in groups:
        h_ref[rows, :] = (_rms_unit(x_ref[rows, :]) * g_ref[...]).astype(BF16)

    for rows in groups:
        cos, sin = cos_ref[rows, :], sin_ref[rows, :]

        def project(col, width):
            return jnp.dot(h_ref[rows, :], w_ref[:, col:col + width],
                           preferred_element_type=F32)

        def rotate(t):
            t1, t2 = t[:, :half], t[:, half:]
            return jnp.concatenate([t1 * cos - t2 * sin, t1 * sin + t2 * cos], axis=1)

        for h in range(RET_HEADS):
            q_ref[h, rows, :] = rotate(project(h * RET_DK, RET_DK)).astype(q_ref.dtype)
            k = rotate(project(k_col + h * RET_DK, RET_DK)) * (RET_DK ** -0.5)
            k_ref[h, rows, :] = k.astype(k_ref.dtype)
            v_ref[h, rows, :] = project(v_col + h * RET_DV, RET_DV).astype(v_ref.dtype)
            gate = project(gate_col + h * RET_DV, RET_DV)
            gate_ref[h, rows, :] = (gate * jax.nn.sigmoid(gate)).astype(gate_ref.dtype)


def _ret_in_proj(x2d, g, w, layer, cos, sin):
    t, d = x2d.shape
    tm = IN_PROJ_ROW_TILE
    tiles_per_seq = cos.shape[0] // tm
    row = lambda i: (i, 0)
    pos = lambda i: (i % tiles_per_seq, 0)
    head_rows = lambda i: (0, i, 0)
    shapes = [(RET_HEADS, t, RET_DK)] * 2 + [(RET_HEADS, t, RET_DV)] * 2
    return pl.pallas_call(
        _ret_in_proj_kernel,
        grid=(t // tm,),
        in_specs=[pl.BlockSpec((tm, d), row),
                  pl.BlockSpec((1, d), lambda i: (0, 0)),
                  _resident_weight(layer, d, RET_IN),
                  pl.BlockSpec((tm, RET_DK // 2), pos),
                  pl.BlockSpec((tm, RET_DK // 2), pos)],
        out_specs=[pl.BlockSpec((RET_HEADS, tm, shape[2]), head_rows) for shape in shapes],
        out_shape=[jax.ShapeDtypeStruct(shape, BF16) for shape in shapes],
        scratch_shapes=[pltpu.VMEM((tm, d), BF16)],
        compiler_params=_params("parallel"),
        name="ret_in_proj",
    )(x2d, g, w, cos, sin)


def _retention_kernel(lg_ref, q_ref, k_ref, v_ref, gate_ref, o_ref, state_ref, *, n_chunks):
    c_len = RET_CHUNK

    @pl.when(pl.program_id(2) == 0)
    def _():
        state_ref[...] = jnp.zeros_like(state_ref)

    lg = lg_ref[pl.program_id(1)]
    ri = lax.broadcasted_iota(jnp.int32, (c_len, c_len), 0)
    ci = lax.broadcasted_iota(jnp.int32, (c_len, c_len), 1)
    diff = (ri - ci).astype(F32)
    decay_mask = jnp.where(diff >= 0, jnp.exp(jnp.maximum(diff, 0.0) * lg), 0.0)
    idx = lax.broadcasted_iota(jnp.int32, (c_len, 1), 0).astype(F32)
    q_decay = jnp.exp((idx + 1.0) * lg)
    k_decay = jnp.exp((c_len - 1.0 - idx) * lg)
    chunk_decay = jnp.exp(jnp.full((1, 1), c_len, F32) * lg)

    def chunk(c, carry):
        rows = pl.ds(pl.multiple_of(c * c_len, c_len), c_len)
        qb = q_ref[rows, :]
        kb = k_ref[rows, :]
        v = v_ref[rows, :]
        inner = lax.dot_general(qb, kb, (((1,), (1,)), ((), ())),
                                preferred_element_type=F32) * decay_mask
        out = jnp.dot(inner.astype(BF16), v, preferred_element_type=F32)
        state = state_ref[...]
        out = out + jnp.dot(qb, state.astype(BF16), preferred_element_type=F32) * q_decay
        kd_t = jnp.transpose(kb.astype(F32) * k_decay).astype(BF16)
        state_ref[...] = state * chunk_decay + jnp.dot(kd_t, v, preferred_element_type=F32)
        o_ref[rows, :] = (_rms_unit(out) * gate_ref[rows, :].astype(F32)).astype(o_ref.dtype)
        return carry

    lax.fori_loop(0, n_chunks, chunk, 0, unroll=True)


def _retention(q, k, v, gate, log_gamma, batch):
    t = q.shape[1]
    s = t // batch
    ts = RET_SEQ_TILE
    tiles_per_seq = s // ts
    head_rows = lambda bi, h, si: (h, bi * tiles_per_seq + si, 0)
    return pl.pallas_call(
        functools.partial(_retention_kernel, n_chunks=ts // RET_CHUNK),
        grid=(batch, RET_HEADS, tiles_per_seq),
        in_specs=[pl.BlockSpec(memory_space=pltpu.SMEM),
                  pl.BlockSpec((None, ts, RET_DK), head_rows),
                  pl.BlockSpec((None, ts, RET_DK), head_rows),
                  pl.BlockSpec((None, ts, RET_DV), head_rows),
                  pl.BlockSpec((None, ts, RET_DV), head_rows)],
        out_specs=pl.BlockSpec((ts, RET_DV), lambda bi, h, si: (bi * tiles_per_seq + si, h)),
        out_shape=jax.ShapeDtypeStruct((t, RET_HEADS * RET_DV), BF16),
        scratch_shapes=[pltpu.VMEM((RET_DK, RET_DV), F32)],
        compiler_params=_params("parallel", "parallel", "arbitrary"),
        name="retention",
    )(log_gamma, q, k, v, gate)


def _bucket_upper_bounds():
    n = np.arange(0, 4 * REL_MAX_DIST, dtype=np.int64)
    nf = np.maximum(n, 1).astype(np.float32)
    ratio = np.log(nf / np.float32(REL_MAX_EXACT)) / np.float32(
        math.log(REL_MAX_DIST / REL_MAX_EXACT)) * np.float32(REL_BUCKETS - REL_MAX_EXACT)
    large = np.minimum(REL_MAX_EXACT + ratio.astype(np.int32), REL_BUCKETS - 1)
    bucket = np.where(n < REL_MAX_EXACT, n, large)
    assert np.all(np.diff(bucket) >= 0) and bucket[REL_MAX_DIST] == REL_BUCKETS - 1
    return [int(n[bucket == b].max()) for b in range(REL_BUCKETS - 1)]


def _bias_tiles_kernel(tab_ref, o_ref, *, tile, upper):
    h = pl.program_id(0)
    key = lax.broadcasted_iota(jnp.int32, (tile, tile), 0)
    qry = lax.broadcasted_iota(jnp.int32, (tile, tile), 1)
    far = tab_ref[h, REL_BUCKETS - 1]
    for t in (BIAS_DIAGONAL, BIAS_PREVIOUS):
        dist = qry - key + t * tile
        val = jnp.zeros((tile, tile), F32)
        for b in range(REL_BUCKETS - 2, -1, -1):
            val = jnp.where(dist <= upper[b], (tab_ref[h, b] - far) * LOG2_E, val)
        o_ref[t] = jnp.where(dist >= 0, val, NEG_BIG)
    o_ref[BIAS_MASKED] = jnp.full((tile, tile), NEG_BIG, F32)


def _bias_tiles(rel_table_t, tile):
    assert tile >= REL_MAX_DIST
    return pl.pallas_call(
        functools.partial(_bias_tiles_kernel, tile=tile, upper=_bucket_upper_bounds()),
        grid=(DIFF_HEADS,),
        in_specs=[pl.BlockSpec(memory_space=pltpu.SMEM)],
        out_specs=pl.BlockSpec((None, 3, tile, tile), lambda h: (h, 0, 0, 0)),
        out_shape=jax.ShapeDtypeStruct((DIFF_HEADS, 3, tile, tile), F32),
        compiler_params=_params("parallel"),
        name="bias_tiles",
    )(rel_table_t)


def _diff_attn_kernel(qs_ref, k_ref, vt_ref, bias_ref, lam_ref, g_ref, o_ref,
                      s_ref, smax_ref, m_ref, acc_ref, *, tile, lambda_init):
    n_tiles = qs_ref.shape[0]
    assert ATTN_UNROLL % 2 == 0 and NEAR_UNROLL % 2 == 0 and n_tiles % NEAR_UNROLL == 0
    ones_rows = jnp.ones((SUM_ROWS, tile), BF16)
    last = n_tiles - 1

    def key_tile(j):
        return k_ref[pl.ds(pl.multiple_of(j * tile, tile), tile), :]

    def value_rows(j):
        return jnp.concatenate([vt_ref[j], ones_rows], axis=0)

    def put_scores(slot, s):
        s_ref[slot] = s
        smax_ref[slot] = jnp.max(s, axis=0, keepdims=True)

    def near_scores(i, base):
        i = jnp.minimum(i, last)
        qs = qs_ref[i]
        bias_t = bias_ref[BIAS_DIAGONAL]
        s = jnp.dot(key_tile(i), qs, preferred_element_type=F32)
        put_scores(base, s + jnp.concatenate([bias_t, bias_t], axis=1))
        bias_t = bias_ref[jnp.where(i >= 1, BIAS_PREVIOUS, BIAS_MASKED)]
        s = jnp.dot(key_tile(jnp.maximum(i - 1, 0)), qs, preferred_element_type=F32)
        put_scores(base + 1, s + jnp.concatenate([bias_t, bias_t], axis=1))

    def near_reduce(i, base):
        m_diag = smax_ref[base]
        p = jnp.exp2(s_ref[base] - m_diag).astype(BF16)
        acc = jnp.dot(value_rows(i), p, preferred_element_type=F32)
        m_new = jnp.maximum(m_diag, smax_ref[base + 1])
        p = jnp.exp2(s_ref[base + 1] - m_new).astype(BF16)
        acc = jnp.exp2(m_diag - m_new) * acc + jnp.dot(
            value_rows(jnp.maximum(i - 1, 0)), p, preferred_element_type=F32)
        acc_ref[i] = acc
        m_ref[i] = m_new

    def near_steps(g, carry):
        for u in range(NEAR_UNROLL):
            i = NEAR_UNROLL * g + u
            near_scores(i + 1, 2 * ((u + 1) % 2))
            near_reduce(i, 2 * (u % 2))
        return carry

    near_scores(jnp.int32(0), 0)
    lax.fori_loop(0, n_tiles // NEAR_UNROLL, near_steps, 0)

    def successor(i, j):
        wraps = j == i - 2
        return jnp.where(wraps, i + 1, i), jnp.where(wraps, 0, j + 1)

    def far_scores(i, j, slot):
        i = jnp.minimum(i, last)
        put_scores(slot, jnp.dot(key_tile(j), qs_ref[i], preferred_element_type=F32))

    def far_reduce(i, j, slot):
        m_prev = m_ref[i]
        m_new = jnp.maximum(m_prev, smax_ref[slot])
        p = jnp.exp2(s_ref[slot] - m_new).astype(BF16)
        acc_ref[i] = jnp.exp2(m_prev - m_new) * acc_ref[i] + jnp.dot(
            value_rows(j), p, preferred_element_type=F32)
        m_ref[i] = m_new

    lam = lam_ref[...]
    lam_full = (jnp.exp(jnp.sum(lam[0:1] * lam[1:2], axis=1, keepdims=True))
                - jnp.exp(jnp.sum(lam[2:3] * lam[3:4], axis=1, keepdims=True))
                + lambda_init)

    def emit(i):
        acc = acc_ref[i]
        inv_l = 1.0 / acc[DIFF_DV:DIFF_DV + 1]
        o_t = (acc[:DIFF_DV, :tile] * inv_l[:, :tile]
               - acc[:DIFF_DV, tile:] * (lam_full * inv_l[:, tile:]))
        scale = lax.rsqrt(jnp.mean(o_t * o_t, axis=0, keepdims=True) + RMS_EPS) * (1.0 - lambda_init)
        o_ref[i] = (o_t * scale * g_ref[...]).astype(o_ref.dtype)

    def far_steps(_, carry):
        i, j = carry
        emit(i - 1)
        for u in range(ATTN_UNROLL):
            i_next, j_next = successor(i, j)
            far_scores(i_next, j_next, (u + 1) % 2)
            far_reduce(i, j, u % 2)
            i, j = i_next, j_next
        return i, j

    far_pairs = [(i, j) for i in range(2, n_tiles) for j in range(i - 1)]
    n_loop = len(far_pairs) // ATTN_UNROLL * ATTN_UNROLL
    emitted = {far_pairs[t][0] - 1 for t in range(0, n_loop, ATTN_UNROLL)}
    first = jnp.int32(2), jnp.int32(0)
    far_scores(*first, 0)
    i, j = lax.fori_loop(0, n_loop // ATTN_UNROLL, far_steps, first)
    for u in range(n_loop, len(far_pairs)):
        i_next, j_next = successor(i, j)
        if u + 1 < len(far_pairs):
            far_scores(i_next, j_next, (u + 1) % 2)
        far_reduce(i, j, u % 2)
        i, j = i_next, j_next
    for i in range(n_tiles):
        if i not in emitted:
            emit(i)


def _diff_attention(k, qs_t, v_t, bias, lam, subln, lambda_init):
    b, s, _ = k.shape
    tile = ATTN_TILE
    nk = s // tile
    return pl.pallas_call(
        functools.partial(_diff_attn_kernel, tile=tile, lambda_init=lambda_init),
        grid=(b, DIFF_HEADS),
        in_specs=[pl.BlockSpec((None, nk, 2 * DIFF_DH, 2 * tile), lambda bi, h: (bi, 0, h, 0)),
                  pl.BlockSpec((None, s, 2 * DIFF_DH), lambda bi, h: (bi, 0, h)),
                  pl.BlockSpec((None, nk, DIFF_DV, tile), lambda bi, h: (bi, 0, h, 0)),
                  pl.BlockSpec((None, 3, tile, tile), lambda bi, h: (h, 0, 0, 0)),
                  pl.BlockSpec((4, DIFF_DH), lambda bi, h: (0, 0)),
                  pl.BlockSpec((DIFF_DV, 1), lambda bi, h: (0, 0))],
        out_specs=pl.BlockSpec((None, nk, DIFF_DV, tile), lambda bi, h: (bi, 0, h, 0)),
        out_shape=jax.ShapeDtypeStruct((b, nk, DIFF_HEADS * DIFF_DV, tile), BF16),
        scratch_shapes=[pltpu.VMEM((4, tile, 2 * tile), F32),
                        pltpu.VMEM((4, 1, 2 * tile), F32),
                        pltpu.VMEM((nk, 1, 2 * tile), F32),
                        pltpu.VMEM((nk, DIFF_DV + SUM_ROWS, 2 * tile), F32)],
        compiler_params=_params("parallel", "parallel"),
        name="diff_attention",
    )(qs_t, k, v_t, bias, lam, subln)


def kernel(x, norm_gains, ret_w_in, ret_w_out, diff_w_in, diff_w_out, diff_lambda,
           diff_subln, rel_bias_table, ffn_w_gate, ffn_w_up, ffn_w_down):
    b, s, d = x.shape
    t = b * s
    x2d = x.reshape(t, d)
    gains = norm_gains.reshape(DEPTH, 4, 1, d)

    half = RET_DK // 2
    inv = ROPE_BASE ** (-jnp.arange(half, dtype=F32) / half)
    ang = jnp.arange(s, dtype=F32)[:, None] * inv[None, :]
    cos, sin = jnp.cos(ang), jnp.sin(ang)
    log_gamma = jnp.log1p(-(2.0 ** (-5.0 - jnp.arange(RET_HEADS, dtype=F32))))

    bias = _bias_tiles(rel_bias_table.T, ATTN_TILE)

    ret_w_in, ret_w_out = ret_w_in.astype(BF16), ret_w_out.astype(BF16)
    diff_w_out = diff_w_out.astype(BF16)
    ffn_w_gate, ffn_w_up, ffn_w_down = (w.astype(BF16) for w in (ffn_w_gate, ffn_w_up, ffn_w_down))
    diff_w_in = diff_w_in.astype(BF16)
    assert 2 * DIFF_HEADS * DIFF_DH == DIFF_HEADS * DIFF_DV

    for i in range(DEPTH):
        g = gains[i]
        j = i // N_MIXERS
        if i % N_MIXERS == 0:
            q, k, v, gate = _ret_in_proj(x2d, g[0], ret_w_in, j, cos, sin)
            y = _retention(q, k, v, gate, log_gamma, b)
            w_out = ret_w_out
        else:
            lambda_init = 0.8 - 0.6 * math.exp(-0.3 * i)
            k, qs_t, v_t = _diff_in_proj(x2d, g[0], diff_w_in, j)
            nk = s // ATTN_TILE
            y = _diff_attention(k.reshape(b, s, -1),
                                qs_t.reshape(b, nk, -1, 2 * ATTN_TILE),
                                v_t.reshape(b, nk, -1, ATTN_TILE),
                                bias, diff_lambda[j], diff_subln[j].reshape(DIFF_DV, 1),
                                lambda_init)
            w_out = diff_w_out
        x2d = _post_mixer(y, w_out, j, x2d, g[1], g[2], ffn_w_gate, ffn_w_up, ffn_w_down, i, g[3])
    return x2d.reshape(b, s, d)
```

```python
import functools
import math

import numpy as np
import jax
import jax.numpy as jnp
from jax import lax
from jax.experimental import pallas as pl
from jax.experimental.pallas import tpu as pltpu

D_MODEL = 1024
DEPTH = 4
N_MIXERS = 2
RMS_EPS = 1e-6

RET_HEADS = 4
RET_DK = D_MODEL // RET_HEADS
RET_DV = 2 * RET_DK
RET_IN = 2 * RET_HEADS * RET_DK + 2 * RET_HEADS * RET_DV
RET_CHUNK = 256
ROPE_BASE = 10000.0

DIFF_HEADS = 8
DIFF_DH = D_MODEL // (2 * DIFF_HEADS)
DIFF_DV = 2 * DIFF_DH
NEG_BIG = -1e30

REL_BUCKETS = 32
REL_MAX_EXACT = REL_BUCKETS // 2
REL_MAX_DIST = 128

D_FF = -(-8 * D_MODEL // (3 * 256)) * 256

VMEM_LIMIT_BYTES = 56 * 1024 * 1024

IN_PROJ_COL_TILE = 512
IN_PROJ_ROW_GROUPS = 2
POST_ROW_TILE = 512
POST_ROW_GROUPS = 2
CAST_BLOCK_ROWS = 32
FFN_COL_TILE = 256
RET_SEQ_TILE = 2048
ATTN_TILE = 512
IN_PROJ_ROW_TILE = ATTN_TILE
ATTN_UNROLL = 8
NEAR_UNROLL = 4
BIAS_DIAGONAL, BIAS_PREVIOUS, BIAS_MASKED = 0, 1, 2
SUM_ROWS = 16
LOG2_E = math.log2(math.e)

F32 = jnp.float32
BF16 = jnp.bfloat16


def _params(*semantics):
    return pltpu.CompilerParams(dimension_semantics=semantics,
                                vmem_limit_bytes=VMEM_LIMIT_BYTES)


def _rms_unit(xf):
    return xf * lax.rsqrt(jnp.mean(xf * xf, axis=-1, keepdims=True) + RMS_EPS)


def _diff_in_proj_kernel(x_ref, g_ref, wq_ref, wk_ref, wv_ref, ok_ref, oq_ref, ov_ref, h_ref):
    tm = x_ref.shape[0]
    groups = [slice(g * tm // IN_PROJ_ROW_GROUPS, (g + 1) * tm // IN_PROJ_ROW_GROUPS)
              for g in range(IN_PROJ_ROW_GROUPS)]
    for rows in groups:
        h_ref[rows, :] = (_rms_unit(x_ref[rows, :]) * g_ref[...]).astype(BF16)
    feature_major = (((0,), (1,)), ((), ()))
    for rows in groups:
        h = h_ref[rows, :]
        for c in range(ok_ref.shape[1] // IN_PROJ_COL_TILE):
            cols = slice(c * IN_PROJ_COL_TILE, (c + 1) * IN_PROJ_COL_TILE)
            ok_ref[rows, cols] = jnp.dot(h, wk_ref[:, cols],
                                         preferred_element_type=F32).astype(ok_ref.dtype)
        ov_ref[:, rows] = lax.dot_general(wv_ref[...], h, feature_major,
                                          preferred_element_type=F32).astype(ov_ref.dtype)
        q_t = lax.dot_general(wq_ref[...], h, feature_major, preferred_element_type=F32)
        q_t = q_t * (DIFF_DH ** -0.5 * LOG2_E)
        feat = lax.broadcasted_iota(jnp.int32, q_t.shape, 0) % (2 * DIFF_DH)
        oq_ref[:, rows] = jnp.where(feat < DIFF_DH, q_t, 0.0).astype(oq_ref.dtype)
        oq_ref[:, slice(tm + rows.start, tm + rows.stop)] = jnp.where(
            feat >= DIFF_DH, q_t, 0.0).astype(oq_ref.dtype)


def _resident_weight(rows, cols, col_block=0):
    return pl.BlockSpec((rows, cols), lambda i: (0, col_block), pipeline_mode=pl.Buffered(1))


def _diff_in_proj(x2d, g, w):
    t, d = x2d.shape
    n = w.shape[1] // 3
    tm = IN_PROJ_ROW_TILE
    tiles = lambda i: (i, 0, 0)
    return pl.pallas_call(
        _diff_in_proj_kernel,
        grid=(t // tm,),
        in_specs=[pl.BlockSpec((tm, d), lambda i: (i, 0)),
                  pl.BlockSpec((1, d), lambda i: (0, 0)),
                  _resident_weight(d, n, 0),
                  _resident_weight(d, n, 1),
                  _resident_weight(d, n, 2)],
        out_specs=[pl.BlockSpec((tm, n), lambda i: (i, 0)),
                   pl.BlockSpec((None, n, 2 * tm), tiles),
                   pl.BlockSpec((None, n, tm), tiles)],
        out_shape=[jax.ShapeDtypeStruct((t, n), BF16),
                   jax.ShapeDtypeStruct((t // tm, n, 2 * tm), BF16),
                   jax.ShapeDtypeStruct((t // tm, n, tm), BF16)],
        scratch_shapes=[pltpu.VMEM((tm, d), BF16)],
        compiler_params=_params("parallel"),
        name="diff_in_proj",
    )(x2d, g, w, w, w)


def _post_mixer_kernel(y_ref, wo_ref, x_ref, gmix_ref, gpre_ref, wg_ref, wu_ref, wd_ref,
                       gpost_ref, *rest, y_feature_major, n_cast):
    cast_in, (o_ref, *cast_out), (h_ref, a_ref) = (rest[:n_cast], rest[n_cast:2 * n_cast + 1],
                                                   rest[2 * n_cast + 1:])
    for src, dst in zip(cast_in, cast_out):
        dst[...] = src[...].astype(dst.dtype)
    tm = o_ref.shape[0]
    groups = [slice(g * tm // POST_ROW_GROUPS, (g + 1) * tm // POST_ROW_GROUPS)
              for g in range(POST_ROW_GROUPS)]
    for rows in groups:
        if y_feature_major:
            m = lax.dot_general(y_ref[:, rows], wo_ref[...], (((0,), (0,)), ((), ())),
                                preferred_element_type=F32)
        else:
            m = jnp.dot(y_ref[rows, :], wo_ref[...], preferred_element_type=F32)
        x1 = x_ref[rows, :] + _rms_unit(m) * gmix_ref[...]
        o_ref[rows, :] = x1
        h_ref[rows, :] = (_rms_unit(x1) * gpre_ref[...]).astype(BF16)
    for rows in groups:
        for c in range(a_ref.shape[1] // FFN_COL_TILE):
            cols = slice(c * FFN_COL_TILE, (c + 1) * FFN_COL_TILE)
            h = h_ref[rows, :]
            gate = jnp.dot(h, wg_ref[:, cols], preferred_element_type=F32)
            up = jnp.dot(h, wu_ref[:, cols], preferred_element_type=F32)
            a_ref[rows, cols] = (gate * jax.nn.sigmoid(gate) * up).astype(BF16)
    for rows in groups:
        f = jnp.dot(a_ref[rows, :], wd_ref[...], preferred_element_type=F32)
        o_ref[rows, :] += _rms_unit(f) * gpost_ref[...]


def _post_mixer(y, w_out, x2d, g_mix, g_pre, w_gate, w_up, w_down, g_post, to_cast=()):
    t, d = x2d.shape
    k = w_out.shape[0]
    ff = w_gate.shape[1]
    tm = POST_ROW_TILE
    steps = t // tm
    row = lambda i: (i, 0)
    fixed = lambda i: (0, 0)
    cast_rows = steps * CAST_BLOCK_ROWS
    cast_args = [w.reshape(w.shape[0], cast_rows, -1) for w, _ in to_cast]
    cast_in_specs = [pl.BlockSpec((None, CAST_BLOCK_ROWS, w.shape[2]),
                                  lambda i, layer=layer: (layer, i, 0))
                     for w, (_, layer) in zip(cast_args, to_cast)]
    cast_out_specs = [pl.BlockSpec((CAST_BLOCK_ROWS, w.shape[2]), row) for w in cast_args]
    cast_out_shapes = [jax.ShapeDtypeStruct(w.shape[1:], BF16) for w in cast_args]
    feature_major = y.ndim == 4
    if feature_major:
        tiles_per_seq = y.shape[1]
        assert y.shape[2:] == (k, tm)
        y_spec = pl.BlockSpec((None, None, k, tm),
                              lambda i: (i // tiles_per_seq, i % tiles_per_seq, 0, 0))
    else:
        y_spec = pl.BlockSpec((tm, k), row)
    out, *cast = pl.pallas_call(
        functools.partial(_post_mixer_kernel, y_feature_major=feature_major,
                          n_cast=len(to_cast)),
        grid=(steps,),
        in_specs=[y_spec,
                  _resident_weight(k, d),
                  pl.BlockSpec((tm, d), row),
                  pl.BlockSpec((1, d), fixed),
                  pl.BlockSpec((1, d), fixed),
                  _resident_weight(d, ff),
                  _resident_weight(d, ff),
                  _resident_weight(ff, d),
                  pl.BlockSpec((1, d), fixed)] + cast_in_specs,
        out_specs=[pl.BlockSpec((tm, d), row)] + cast_out_specs,
        out_shape=[jax.ShapeDtypeStruct((t, d), F32)] + cast_out_shapes,
        scratch_shapes=[pltpu.VMEM((tm, d), BF16), pltpu.VMEM((tm, ff), BF16)],
        compiler_params=_params("parallel"),
        name="post_mixer",
    )(y, w_out, x2d, g_mix, g_pre, w_gate, w_up, w_down, g_post, *cast_args)
    return out, [c.reshape(w.shape[1:]) for c, (w, _) in zip(cast, to_cast)]


def _ret_in_proj_kernel(x_ref, g_ref, w_ref, cos_ref, sin_ref, q_ref, k_ref, v_ref, gate_ref,
                        h_ref):
    half = RET_DK // 2
    k_col, v_col, gate_col = (RET_HEADS * RET_DK, 2 * RET_HEADS * RET_DK,
                              2 * RET_HEADS * RET_DK + RET_HEADS * RET_DV)
    tm = x_ref.shape[0]
    groups = [slice(g * tm // IN_PROJ_ROW_GROUPS, (g + 1) * tm // IN_PROJ_ROW_GROUPS)
              for g in range(IN_PROJ_ROW_GROUPS)]
    for rows in groups:
        h_ref[rows, :] = (_rms_unit(x_ref[rows, :]) * g_ref[...]).astype(BF16)

    for rows in groups:
        cos, sin = cos_ref[rows, :], sin_ref[rows, :]

        def project(col, width):
            return jnp.dot(h_ref[rows, :], w_ref[:, col:col + width],
                           preferred_element_type=F32)

        def rotate(t):
            t1, t2 = t[:, :half], t[:, half:]
            return jnp.concatenate([t1 * cos - t2 * sin, t1 * sin + t2 * cos], axis=1)

        for h in range(RET_HEADS):
            q_ref[h, rows, :] = rotate(project(h * RET_DK, RET_DK)).astype(q_ref.dtype)
            k = rotate(project(k_col + h * RET_DK, RET_DK)) * (RET_DK ** -0.5)
            k_ref[h, rows, :] = k.astype(k_ref.dtype)
            v_ref[h, rows, :] = project(v_col + h * RET_DV, RET_DV).astype(v_ref.dtype)
            gate = project(gate_col + h * RET_DV, RET_DV)
            gate_ref[h, rows, :] = (gate * jax.nn.sigmoid(gate)).astype(gate_ref.dtype)


def _ret_in_proj(x2d, g, w, cos, sin):
    t, d = x2d.shape
    tm = IN_PROJ_ROW_TILE
    tiles_per_seq = cos.shape[0] // tm
    row = lambda i: (i, 0)
    pos = lambda i: (i % tiles_per_seq, 0)
    head_rows = lambda i: (0, i, 0)
    shapes = [(RET_HEADS, t, RET_DK)] * 2 + [(RET_HEADS, t, RET_DV)] * 2
    return pl.pallas_call(
        _ret_in_proj_kernel,
        grid=(t // tm,),
        in_specs=[pl.BlockSpec((tm, d), row),
                  pl.BlockSpec((1, d), lambda i: (0, 0)),
                  _resident_weight(d, RET_IN),
                  pl.BlockSpec((tm, RET_DK // 2), pos),
                  pl.BlockSpec((tm, RET_DK // 2), pos)],
        out_specs=[pl.BlockSpec((RET_HEADS, tm, shape[2]), head_rows) for shape in shapes],
        out_shape=[jax.ShapeDtypeStruct(shape, BF16) for shape in shapes],
        scratch_shapes=[pltpu.VMEM((tm, d), BF16)],
        compiler_params=_params("parallel"),
        name="ret_in_proj",
    )(x2d, g, w, cos, sin)


def _retention_kernel(lg_ref, q_ref, k_ref, v_ref, gate_ref, o_ref, state_ref, *, n_chunks):
    c_len = RET_CHUNK

    @pl.when(pl.program_id(2) == 0)
    def _():
        state_ref[...] = jnp.zeros_like(state_ref)

    lg = lg_ref[pl.program_id(1)]
    ri = lax.broadcasted_iota(jnp.int32, (c_len, c_len), 0)
    ci = lax.broadcasted_iota(jnp.int32, (c_len, c_len), 1)
    diff = (ri - ci).astype(F32)
    decay_mask = jnp.where(diff >= 0, jnp.exp(jnp.maximum(diff, 0.0) * lg), 0.0)
    idx = lax.broadcasted_iota(jnp.int32, (c_len, 1), 0).astype(F32)
    q_decay = jnp.exp((idx + 1.0) * lg)
    k_decay = jnp.exp((c_len - 1.0 - idx) * lg)
    chunk_decay = jnp.exp(jnp.full((1, 1), c_len, F32) * lg)

    def chunk(c, carry):
        rows = pl.ds(pl.multiple_of(c * c_len, c_len), c_len)
        qb = q_ref[rows, :]
        kb = k_ref[rows, :]
        v = v_ref[rows, :]
        inner = lax.dot_general(qb, kb, (((1,), (1,)), ((), ())),
                                preferred_element_type=F32) * decay_mask
        out = jnp.dot(inner.astype(BF16), v, preferred_element_type=F32)
        state = state_ref[...]
        out = out + jnp.dot(qb, state.astype(BF16), preferred_element_type=F32) * q_decay
        kd_t = jnp.transpose(kb.astype(F32) * k_decay).astype(BF16)
        state_ref[...] = state * chunk_decay + jnp.dot(kd_t, v, preferred_element_type=F32)
        o_ref[rows, :] = (_rms_unit(out) * gate_ref[rows, :].astype(F32)).astype(o_ref.dtype)
        return carry

    lax.fori_loop(0, n_chunks, chunk, 0, unroll=True)


def _retention(q, k, v, gate, log_gamma, batch):
    t = q.shape[1]
    s = t // batch
    ts = RET_SEQ_TILE
    tiles_per_seq = s // ts
    head_rows = lambda bi, h, si: (h, bi * tiles_per_seq + si, 0)
    return pl.pallas_call(
        functools.partial(_retention_kernel, n_chunks=ts // RET_CHUNK),
        grid=(batch, RET_HEADS, tiles_per_seq),
        in_specs=[pl.BlockSpec(memory_space=pltpu.SMEM),
                  pl.BlockSpec((None, ts, RET_DK), head_rows),
                  pl.BlockSpec((None, ts, RET_DK), head_rows),
                  pl.BlockSpec((None, ts, RET_DV), head_rows),
                  pl.BlockSpec((None, ts, RET_DV), head_rows)],
        out_specs=pl.BlockSpec((ts, RET_DV), lambda bi, h, si: (bi * tiles_per_seq + si, h)),
        out_shape=jax.ShapeDtypeStruct((t, RET_HEADS * RET_DV), BF16),
        scratch_shapes=[pltpu.VMEM((RET_DK, RET_DV), F32)],
        compiler_params=_params("parallel", "parallel", "arbitrary"),
        name="retention",
    )(log_gamma, q, k, v, gate)


def _bucket_upper_bounds():
    n = np.arange(0, 4 * REL_MAX_DIST, dtype=np.int64)
    nf = np.maximum(n, 1).astype(np.float32)
    ratio = np.log(nf / np.float32(REL_MAX_EXACT)) / np.float32(
        math.log(REL_MAX_DIST / REL_MAX_EXACT)) * np.float32(REL_BUCKETS - REL_MAX_EXACT)
    large = np.minimum(REL_MAX_EXACT + ratio.astype(np.int32), REL_BUCKETS - 1)
    bucket = np.where(n < REL_MAX_EXACT, n, large)
    assert np.all(np.diff(bucket) >= 0) and bucket[REL_MAX_DIST] == REL_BUCKETS - 1
    return [int(n[bucket == b].max()) for b in range(REL_BUCKETS - 1)]


def _bias_tiles_kernel(tab_ref, o_ref, *, tile, upper):
    h = pl.program_id(0)
    key = lax.broadcasted_iota(jnp.int32, (tile, tile), 0)
    qry = lax.broadcasted_iota(jnp.int32, (tile, tile), 1)
    far = tab_ref[h, REL_BUCKETS - 1]
    for t in (BIAS_DIAGONAL, BIAS_PREVIOUS):
        dist = qry - key + t * tile
        val = jnp.zeros((tile, tile), F32)
        for b in range(REL_BUCKETS - 2, -1, -1):
            val = jnp.where(dist <= upper[b], (tab_ref[h, b] - far) * LOG2_E, val)
        o_ref[t] = jnp.where(dist >= 0, val, NEG_BIG)
    o_ref[BIAS_MASKED] = jnp.full((tile, tile), NEG_BIG, F32)


def _bias_tiles(rel_table_t, tile):
    assert tile >= REL_MAX_DIST
    return pl.pallas_call(
        functools.partial(_bias_tiles_kernel, tile=tile, upper=_bucket_upper_bounds()),
        grid=(DIFF_HEADS,),
        in_specs=[pl.BlockSpec(memory_space=pltpu.SMEM)],
        out_specs=pl.BlockSpec((None, 3, tile, tile), lambda h: (h, 0, 0, 0)),
        out_shape=jax.ShapeDtypeStruct((DIFF_HEADS, 3, tile, tile), F32),
        compiler_params=_params("parallel"),
        name="bias_tiles",
    )(rel_table_t)


def _diff_attn_kernel(qs_ref, k_ref, vt_ref, bias_ref, lam_ref, g_ref, o_ref,
                      s_ref, smax_ref, m_ref, acc_ref, *, tile, lambda_init):
    n_tiles = qs_ref.shape[0]
    assert ATTN_UNROLL % 2 == 0 and NEAR_UNROLL % 2 == 0 and n_tiles % NEAR_UNROLL == 0
    ones_rows = jnp.ones((SUM_ROWS, tile), BF16)
    last = n_tiles - 1

    def key_tile(j):
        return k_ref[pl.ds(pl.multiple_of(j * tile, tile), tile), :]

    def value_rows(j):
        return jnp.concatenate([vt_ref[j], ones_rows], axis=0)

    def put_scores(slot, s):
        s_ref[slot] = s
        smax_ref[slot] = jnp.max(s, axis=0, keepdims=True)

    def near_scores(i, base):
        i = jnp.minimum(i, last)
        qs = qs_ref[i]
        bias_t = bias_ref[BIAS_DIAGONAL]
        s = jnp.dot(key_tile(i), qs, preferred_element_type=F32)
        put_scores(base, s + jnp.concatenate([bias_t, bias_t], axis=1))
        bias_t = bias_ref[jnp.where(i >= 1, BIAS_PREVIOUS, BIAS_MASKED)]
        s = jnp.dot(key_tile(jnp.maximum(i - 1, 0)), qs, preferred_element_type=F32)
        put_scores(base + 1, s + jnp.concatenate([bias_t, bias_t], axis=1))

    def near_reduce(i, base):
        m_diag = smax_ref[base]
        p = jnp.exp2(s_ref[base] - m_diag).astype(BF16)
        acc = jnp.dot(value_rows(i), p, preferred_element_type=F32)
        m_new = jnp.maximum(m_diag, smax_ref[base + 1])
        p = jnp.exp2(s_ref[base + 1] - m_new).astype(BF16)
        acc = jnp.exp2(m_diag - m_new) * acc + jnp.dot(
            value_rows(jnp.maximum(i - 1, 0)), p, preferred_element_type=F32)
        acc_ref[i] = acc
        m_ref[i] = m_new

    def near_steps(g, carry):
        for u in range(NEAR_UNROLL):
            i = NEAR_UNROLL * g + u
            near_scores(i + 1, 2 * ((u + 1) % 2))
            near_reduce(i, 2 * (u % 2))
        return carry

    near_scores(jnp.int32(0), 0)
    lax.fori_loop(0, n_tiles // NEAR_UNROLL, near_steps, 0)

    def successor(i, j):
        wraps = j == i - 2
        return jnp.where(wraps, i + 1, i), jnp.where(wraps, 0, j + 1)

    def far_scores(i, j, slot):
        i = jnp.minimum(i, last)
        put_scores(slot, jnp.dot(key_tile(j), qs_ref[i], preferred_element_type=F32))

    def far_reduce(i, j, slot):
        m_prev = m_ref[i]
        m_new = jnp.maximum(m_prev, smax_ref[slot])
        p = jnp.exp2(s_ref[slot] - m_new).astype(BF16)
        acc_ref[i] = jnp.exp2(m_prev - m_new) * acc_ref[i] + jnp.dot(
            value_rows(j), p, preferred_element_type=F32)
        m_ref[i] = m_new

    lam = lam_ref[...]
    lam_full = (jnp.exp(jnp.sum(lam[0:1] * lam[1:2], axis=1, keepdims=True))
                - jnp.exp(jnp.sum(lam[2:3] * lam[3:4], axis=1, keepdims=True))
                + lambda_init)

    def emit(i):
        acc = acc_ref[i]
        inv_l = 1.0 / acc[DIFF_DV:DIFF_DV + 1]
        o_t = (acc[:DIFF_DV, :tile] * inv_l[:, :tile]
               - acc[:DIFF_DV, tile:] * (lam_full * inv_l[:, tile:]))
        scale = lax.rsqrt(jnp.mean(o_t * o_t, axis=0, keepdims=True) + RMS_EPS) * (1.0 - lambda_init)
        o_ref[i] = (o_t * scale * g_ref[...]).astype(o_ref.dtype)

    def far_steps(_, carry):
        i, j = carry
        emit(i - 1)
        for u in range(ATTN_UNROLL):
            i_next, j_next = successor(i, j)
            far_scores(i_next, j_next, (u + 1) % 2)
            far_reduce(i, j, u % 2)
            i, j = i_next, j_next
        return i, j

    far_pairs = [(i, j) for i in range(2, n_tiles) for j in range(i - 1)]
    n_loop = len(far_pairs) // ATTN_UNROLL * ATTN_UNROLL
    emitted = {far_pairs[t][0] - 1 for t in range(0, n_loop, ATTN_UNROLL)}
    first = jnp.int32(2), jnp.int32(0)
    far_scores(*first, 0)
    i, j = lax.fori_loop(0, n_loop // ATTN_UNROLL, far_steps, first)
    for u in range(n_loop, len(far_pairs)):
        i_next, j_next = successor(i, j)
        if u + 1 < len(far_pairs):
            far_scores(i_next, j_next, (u + 1) % 2)
        far_reduce(i, j, u % 2)
        i, j = i_next, j_next
    for i in range(n_tiles):
        if i not in emitted:
            emit(i)


def _diff_attention(k, qs_t, v_t, bias, lam, subln, lambda_init):
    b, s, _ = k.shape
    tile = ATTN_TILE
    nk = s // tile
    return pl.pallas_call(
        functools.partial(_diff_attn_kernel, tile=tile, lambda_init=lambda_init),
        grid=(b, DIFF_HEADS),
        in_specs=[pl.BlockSpec((None, nk, 2 * DIFF_DH, 2 * tile), lambda bi, h: (bi, 0, h, 0)),
                  pl.BlockSpec((None, s, 2 * DIFF_DH), lambda bi, h: (bi, 0, h)),
                  pl.BlockSpec((None, nk, DIFF_DV, tile), lambda bi, h: (bi, 0, h, 0)),
                  pl.BlockSpec((None, 3, tile, tile), lambda bi, h: (h, 0, 0, 0)),
                  pl.BlockSpec((4, DIFF_DH), lambda bi, h: (0, 0)),
                  pl.BlockSpec((DIFF_DV, 1), lambda bi, h: (0, 0))],
        out_specs=pl.BlockSpec((None, nk, DIFF_DV, tile), lambda bi, h: (bi, 0, h, 0)),
        out_shape=jax.ShapeDtypeStruct((b, nk, DIFF_HEADS * DIFF_DV, tile), BF16),
        scratch_shapes=[pltpu.VMEM((4, tile, 2 * tile), F32),
                        pltpu.VMEM((4, 1, 2 * tile), F32),
                        pltpu.VMEM((nk, 1, 2 * tile), F32),
                        pltpu.VMEM((nk, DIFF_DV + SUM_ROWS, 2 * tile), F32)],
        compiler_params=_params("parallel", "parallel"),
        name="diff_attention",
    )(qs_t, k, v_t, bias, lam, subln)


def kernel(x, norm_gains, ret_w_in, ret_w_out, diff_w_in, diff_w_out, diff_lambda,
           diff_subln, rel_bias_table, ffn_w_gate, ffn_w_up, ffn_w_down):
    b, s, d = x.shape
    t = b * s
    x2d = x.reshape(t, d)
    gains = norm_gains.reshape(DEPTH, 4, 1, d)

    half = RET_DK // 2
    inv = ROPE_BASE ** (-jnp.arange(half, dtype=F32) / half)
    ang = jnp.arange(s, dtype=F32)[:, None] * inv[None, :]
    cos, sin = jnp.cos(ang), jnp.sin(ang)
    log_gamma = jnp.log1p(-(2.0 ** (-5.0 - jnp.arange(RET_HEADS, dtype=F32))))

    bias = _bias_tiles(rel_bias_table.T, ATTN_TILE)

    assert 2 * DIFF_HEADS * DIFF_DH == DIFF_HEADS * DIFF_DV

    def f32_weights(layer):
        mixer = layer // N_MIXERS
        w_in, w_out = (ret_w_in, ret_w_out) if layer % N_MIXERS == 0 else (diff_w_in, diff_w_out)
        return [(w_in, mixer), (w_out, mixer), (ffn_w_gate, layer), (ffn_w_up, layer),
                (ffn_w_down, layer)]

    weights = [w[index].astype(BF16) for w, index in f32_weights(0)]

    for i in range(DEPTH):
        g = gains[i]
        w_in, w_out, w_gate, w_up, w_down = weights
        if i % N_MIXERS == 0:
            q, k, v, gate = _ret_in_proj(x2d, g[0], w_in, cos, sin)
            y = _retention(q, k, v, gate, log_gamma, b)
        else:
            lambda_init = 0.8 - 0.6 * math.exp(-0.3 * i)
            j = i // N_MIXERS
            k, qs_t, v_t = _diff_in_proj(x2d, g[0], w_in)
            nk = s // ATTN_TILE
            y = _diff_attention(k.reshape(b, s, -1),
                                qs_t.reshape(b, nk, -1, 2 * ATTN_TILE),
                                v_t.reshape(b, nk, -1, ATTN_TILE),
                                bias, diff_lambda[j], diff_subln[j].reshape(DIFF_DV, 1),
                                lambda_init)
        x2d, weights = _post_mixer(y, w_out, x2d, g[1], g[2], w_gate, w_up, w_down, g[3],
                                   to_cast=f32_weights(i + 1) if i + 1 < DEPTH else ())
    return x2d.reshape(b, s, d)
```

```python
import functools
import math

import numpy as np
import jax
import jax.numpy as jnp
from jax import lax
from jax.experimental import pallas as pl
from jax.experimental.pallas import tpu as pltpu

D_MODEL = 1024
DEPTH = 4
N_MIXERS = 2
RMS_EPS = 1e-6

RET_HEADS = 4
RET_DK = D_MODEL // RET_HEADS
RET_DV = 2 * RET_DK
RET_IN = 2 * RET_HEADS * RET_DK + 2 * RET_HEADS * RET_DV
RET_CHUNK = 256
ROPE_BASE = 10000.0

DIFF_HEADS = 8
DIFF_DH = D_MODEL // (2 * DIFF_HEADS)
DIFF_DV = 2 * DIFF_DH
NEG_BIG = -1e30

REL_BUCKETS = 32
REL_MAX_EXACT = REL_BUCKETS // 2
REL_MAX_DIST = 128

D_FF = -(-8 * D_MODEL // (3 * 256)) * 256

VMEM_LIMIT_BYTES = 56 * 1024 * 1024

IN_PROJ_COL_TILE = 512
IN_PROJ_ROW_GROUPS = 2
POST_ROW_TILE = 512
POST_ROW_GROUPS = 2
BF16_SUBLANES = 16
FFN_COL_TILE = 256
RET_SEQ_TILE = 2048
ATTN_TILE = 512
IN_PROJ_ROW_TILE = ATTN_TILE
ATTN_UNROLL = 8
NEAR_UNROLL = 4
BIAS_DIAGONAL, BIAS_PREVIOUS, BIAS_MASKED = 0, 1, 2
SUM_ROWS = 16
LOG2_E = math.log2(math.e)

F32 = jnp.float32
BF16 = jnp.bfloat16


def _params(*semantics):
    return pltpu.CompilerParams(dimension_semantics=semantics,
                                vmem_limit_bytes=VMEM_LIMIT_BYTES)


def _rms_unit(xf):
    return xf * lax.rsqrt(jnp.mean(xf * xf, axis=-1, keepdims=True) + RMS_EPS)


def _diff_in_proj_kernel(x_ref, g_ref, wq_ref, wk_ref, wv_ref, ok_ref, oq_ref, ov_ref, h_ref):
    tm = x_ref.shape[0]
    groups = [slice(g * tm // IN_PROJ_ROW_GROUPS, (g + 1) * tm // IN_PROJ_ROW_GROUPS)
              for g in range(IN_PROJ_ROW_GROUPS)]
    for rows in groups:
        h_ref[rows, :] = (_rms_unit(x_ref[rows, :]) * g_ref[...]).astype(BF16)
    feature_major = (((0,), (1,)), ((), ()))
    for rows in groups:
        h = h_ref[rows, :]
        for c in range(ok_ref.shape[1] // IN_PROJ_COL_TILE):
            cols = slice(c * IN_PROJ_COL_TILE, (c + 1) * IN_PROJ_COL_TILE)
            ok_ref[rows, cols] = jnp.dot(h, wk_ref[:, cols],
                                         preferred_element_type=F32).astype(ok_ref.dtype)
        ov_ref[:, rows] = lax.dot_general(wv_ref[...], h, feature_major,
                                          preferred_element_type=F32).astype(ov_ref.dtype)
        q_t = lax.dot_general(wq_ref[...], h, feature_major, preferred_element_type=F32)
        q_t = q_t * (DIFF_DH ** -0.5 * LOG2_E)
        feat = lax.broadcasted_iota(jnp.int32, q_t.shape, 0) % (2 * DIFF_DH)
        oq_ref[:, rows] = jnp.where(feat < DIFF_DH, q_t, 0.0).astype(oq_ref.dtype)
        oq_ref[:, slice(tm + rows.start, tm + rows.stop)] = jnp.where(
            feat >= DIFF_DH, q_t, 0.0).astype(oq_ref.dtype)


def _resident_weight(rows, cols, col_block=0):
    return pl.BlockSpec((rows, cols), lambda i: (0, col_block), pipeline_mode=pl.Buffered(1))


def _diff_in_proj(x2d, g, w):
    t, d = x2d.shape
    n = w.shape[1] // 3
    tm = IN_PROJ_ROW_TILE
    tiles = lambda i: (i, 0, 0)
    return pl.pallas_call(
        _diff_in_proj_kernel,
        grid=(t // tm,),
        in_specs=[pl.BlockSpec((tm, d), lambda i: (i, 0)),
                  pl.BlockSpec((1, d), lambda i: (0, 0)),
                  _resident_weight(d, n, 0),
                  _resident_weight(d, n, 1),
                  _resident_weight(d, n, 2)],
        out_specs=[pl.BlockSpec((tm, n), lambda i: (i, 0)),
                   pl.BlockSpec((None, n, 2 * tm), tiles),
                   pl.BlockSpec((None, n, tm), tiles)],
        out_shape=[jax.ShapeDtypeStruct((t, n), BF16),
                   jax.ShapeDtypeStruct((t // tm, n, 2 * tm), BF16),
                   jax.ShapeDtypeStruct((t // tm, n, tm), BF16)],
        scratch_shapes=[pltpu.VMEM((tm, d), BF16)],
        compiler_params=_params("parallel"),
        name="diff_in_proj",
    )(x2d, g, w, w, w)


def _post_mixer_kernel(y_ref, wo_ref, x_ref, gmix_ref, gpre_ref, wg_ref, wu_ref, wd_ref,
                       gpost_ref, *rest, y_feature_major, n_cast):
    cast_in, (o_ref, *cast_out), (h_ref, a_ref) = (rest[:n_cast], rest[n_cast:2 * n_cast + 1],
                                                   rest[2 * n_cast + 1:])
    for src, dst in zip(cast_in, cast_out):
        dst[...] = src[...].astype(dst.dtype)
    tm = o_ref.shape[0]
    groups = [slice(g * tm // POST_ROW_GROUPS, (g + 1) * tm // POST_ROW_GROUPS)
              for g in range(POST_ROW_GROUPS)]
    for rows in groups:
        if y_feature_major:
            m = lax.dot_general(y_ref[:, rows], wo_ref[...], (((0,), (0,)), ((), ())),
                                preferred_element_type=F32)
        else:
            m = jnp.dot(y_ref[rows, :], wo_ref[...], preferred_element_type=F32)
        x1 = x_ref[rows, :] + _rms_unit(m) * gmix_ref[...]
        o_ref[rows, :] = x1
        h_ref[rows, :] = (_rms_unit(x1) * gpre_ref[...]).astype(BF16)
    for rows in groups:
        for c in range(a_ref.shape[1] // FFN_COL_TILE):
            cols = slice(c * FFN_COL_TILE, (c + 1) * FFN_COL_TILE)
            h = h_ref[rows, :]
            gate = jnp.dot(h, wg_ref[:, cols], preferred_element_type=F32)
            up = jnp.dot(h, wu_ref[:, cols], preferred_element_type=F32)
            a_ref[rows, cols] = (gate * jax.nn.sigmoid(gate) * up).astype(BF16)
    for rows in groups:
        f = jnp.dot(a_ref[rows, :], wd_ref[...], preferred_element_type=F32)
        o_ref[rows, :] += _rms_unit(f) * gpost_ref[...]


def _post_mixer(y, w_out, x2d, g_mix, g_pre, w_gate, w_up, w_down, g_post, to_cast=()):
    t, d = x2d.shape
    k = w_out.shape[0]
    ff = w_gate.shape[1]
    tm = POST_ROW_TILE
    steps = t // tm
    row = lambda i: (i, 0)
    fixed = lambda i: (0, 0)
    cast_args = [w for w, _ in to_cast]
    cast_in_specs, cast_out_specs = [], []
    for w, layer in to_cast:
        n_blocks = max(n for n in range(1, steps + 1)
                       if w.shape[1] % n == 0 and (w.shape[1] // n) % BF16_SUBLANES == 0)
        block = (w.shape[1] // n_blocks, w.shape[2])
        cast_in_specs.append(pl.BlockSpec(
            (None,) + block, lambda i, layer=layer, n=n_blocks: (layer, jnp.minimum(i, n - 1), 0)))
        cast_out_specs.append(pl.BlockSpec(
            block, lambda i, n=n_blocks: (jnp.minimum(i, n - 1), 0)))
    cast_out_shapes = [jax.ShapeDtypeStruct(w.shape[1:], BF16) for w in cast_args]
    feature_major = y.ndim == 4
    if feature_major:
        tiles_per_seq = y.shape[1]
        assert y.shape[2:] == (k, tm)
        y_spec = pl.BlockSpec((None, None, k, tm),
                              lambda i: (i // tiles_per_seq, i % tiles_per_seq, 0, 0))
    else:
        y_spec = pl.BlockSpec((tm, k), row)
    out, *cast = pl.pallas_call(
        functools.partial(_post_mixer_kernel, y_feature_major=feature_major,
                          n_cast=len(to_cast)),
        grid=(steps,),
        in_specs=[y_spec,
                  _resident_weight(k, d),
                  pl.BlockSpec((tm, d), row),
                  pl.BlockSpec((1, d), fixed),
                  pl.BlockSpec((1, d), fixed),
                  _resident_weight(d, ff),
                  _resident_weight(d, ff),
                  _resident_weight(ff, d),
                  pl.BlockSpec((1, d), fixed)] + cast_in_specs,
        out_specs=[pl.BlockSpec((tm, d), row)] + cast_out_specs,
        out_shape=[jax.ShapeDtypeStruct((t, d), F32)] + cast_out_shapes,
        scratch_shapes=[pltpu.VMEM((tm, d), BF16), pltpu.VMEM((tm, ff), BF16)],
        compiler_params=_params("arbitrary"),
        name="post_mixer",
    )(y, w_out, x2d, g_mix, g_pre, w_gate, w_up, w_down, g_post, *cast_args)
    return out, cast


def _ret_in_proj_kernel(x_ref, g_ref, w_ref, cos_ref, sin_ref, q_ref, k_ref, v_ref, gate_ref,
                        h_ref):
    half = RET_DK // 2
    k_col, v_col, gate_col = (RET_HEADS * RET_DK, 2 * RET_HEADS * RET_DK,
                              2 * RET_HEADS * RET_DK + RET_HEADS * RET_DV)
    tm = x_ref.shape[0]
    groups = [slice(g * tm // IN_PROJ_ROW_GROUPS, (g + 1) * tm // IN_PROJ_ROW_GROUPS)
              for g in range(IN_PROJ_ROW_GROUPS)]
    for rows in groups:
        h_ref[rows, :] = (_rms_unit(x_ref[rows, :]) * g_ref[...]).astype(BF16)

    for rows in groups:
        cos, sin = cos_ref[rows, :], sin_ref[rows, :]

        def project(col, width):
            return jnp.dot(h_ref[rows, :], w_ref[:, col:col + width],
                           preferred_element_type=F32)

        def rotate(t):
            t1, t2 = t[:, :half], t[:, half:]
            return jnp.concatenate([t1 * cos - t2 * sin, t1 * sin + t2 * cos], axis=1)

        for h in range(RET_HEADS):
            q_ref[h, rows, :] = rotate(project(h * RET_DK, RET_DK)).astype(q_ref.dtype)
            k = rotate(project(k_col + h * RET_DK, RET_DK)) * (RET_DK ** -0.5)
            k_ref[h, rows, :] = k.astype(k_ref.dtype)
            v_ref[h, rows, :] = project(v_col + h * RET_DV, RET_DV).astype(v_ref.dtype)
            gate = project(gate_col + h * RET_DV, RET_DV)
            gate_ref[h, rows, :] = (gate * jax.nn.sigmoid(gate)).astype(gate_ref.dtype)


def _ret_in_proj(x2d, g, w, cos, sin):
    t, d = x2d.shape
    tm = IN_PROJ_ROW_TILE
    tiles_per_seq = cos.shape[0] // tm
    row = lambda i: (i, 0)
    pos = lambda i: (i % tiles_per_seq, 0)
    head_rows = lambda i: (0, i, 0)
    shapes = [(RET_HEADS, t, RET_DK)] * 2 + [(RET_HEADS, t, RET_DV)] * 2
    return pl.pallas_call(
        _ret_in_proj_kernel,
        grid=(t // tm,),
        in_specs=[pl.BlockSpec((tm, d), row),
                  pl.BlockSpec((1, d), lambda i: (0, 0)),
                  _resident_weight(d, RET_IN),
                  pl.BlockSpec((tm, RET_DK // 2), pos),
                  pl.BlockSpec((tm, RET_DK // 2), pos)],
        out_specs=[pl.BlockSpec((RET_HEADS, tm, shape[2]), head_rows) for shape in shapes],
        out_shape=[jax.ShapeDtypeStruct(shape, BF16) for shape in shapes],
        scratch_shapes=[pltpu.VMEM((tm, d), BF16)],
        compiler_params=_params("parallel"),
        name="ret_in_proj",
    )(x2d, g, w, cos, sin)


def _retention_kernel(lg_ref, q_ref, k_ref, v_ref, gate_ref, o_ref, state_ref, *, n_chunks):
    c_len = RET_CHUNK

    @pl.when(pl.program_id(2) == 0)
    def _():
        state_ref[...] = jnp.zeros_like(state_ref)

    lg = lg_ref[pl.program_id(1)]
    ri = lax.broadcasted_iota(jnp.int32, (c_len, c_len), 0)
    ci = lax.broadcasted_iota(jnp.int32, (c_len, c_len), 1)
    diff = (ri - ci).astype(F32)
    decay_mask = jnp.where(diff >= 0, jnp.exp(jnp.maximum(diff, 0.0) * lg), 0.0)
    idx = lax.broadcasted_iota(jnp.int32, (c_len, 1), 0).astype(F32)
    q_decay = jnp.exp((idx + 1.0) * lg)
    k_decay = jnp.exp((c_len - 1.0 - idx) * lg)
    chunk_decay = jnp.exp(jnp.full((1, 1), c_len, F32) * lg)

    def chunk(c, carry):
        rows = pl.ds(pl.multiple_of(c * c_len, c_len), c_len)
        qb = q_ref[rows, :]
        kb = k_ref[rows, :]
        v = v_ref[rows, :]
        inner = lax.dot_general(qb, kb, (((1,), (1,)), ((), ())),
                                preferred_element_type=F32) * decay_mask
        out = jnp.dot(inner.astype(BF16), v, preferred_element_type=F32)
        state = state_ref[...]
        out = out + jnp.dot(qb, state.astype(BF16), preferred_element_type=F32) * q_decay
        kd_t = jnp.transpose(kb.astype(F32) * k_decay).astype(BF16)
        state_ref[...] = state * chunk_decay + jnp.dot(kd_t, v, preferred_element_type=F32)
        o_ref[rows, :] = (_rms_unit(out) * gate_ref[rows, :].astype(F32)).astype(o_ref.dtype)
        return carry

    lax.fori_loop(0, n_chunks, chunk, 0, unroll=True)


def _retention(q, k, v, gate, log_gamma, batch):
    t = q.shape[1]
    s = t // batch
    ts = RET_SEQ_TILE
    tiles_per_seq = s // ts
    head_rows = lambda bi, h, si: (h, bi * tiles_per_seq + si, 0)
    return pl.pallas_call(
        functools.partial(_retention_kernel, n_chunks=ts // RET_CHUNK),
        grid=(batch, RET_HEADS, tiles_per_seq),
        in_specs=[pl.BlockSpec(memory_space=pltpu.SMEM),
                  pl.BlockSpec((None, ts, RET_DK), head_rows),
                  pl.BlockSpec((None, ts, RET_DK), head_rows),
                  pl.BlockSpec((None, ts, RET_DV), head_rows),
                  pl.BlockSpec((None, ts, RET_DV), head_rows)],
        out_specs=pl.BlockSpec((ts, RET_DV), lambda bi, h, si: (bi * tiles_per_seq + si, h)),
        out_shape=jax.ShapeDtypeStruct((t, RET_HEADS * RET_DV), BF16),
        scratch_shapes=[pltpu.VMEM((RET_DK, RET_DV), F32)],
        compiler_params=_params("parallel", "parallel", "arbitrary"),
        name="retention",
    )(log_gamma, q, k, v, gate)


def _bucket_upper_bounds():
    n = np.arange(0, 4 * REL_MAX_DIST, dtype=np.int64)
    nf = np.maximum(n, 1).astype(np.float32)
    ratio = np.log(nf / np.float32(REL_MAX_EXACT)) / np.float32(
        math.log(REL_MAX_DIST / REL_MAX_EXACT)) * np.float32(REL_BUCKETS - REL_MAX_EXACT)
    large = np.minimum(REL_MAX_EXACT + ratio.astype(np.int32), REL_BUCKETS - 1)
    bucket = np.where(n < REL_MAX_EXACT, n, large)
    assert np.all(np.diff(bucket) >= 0) and bucket[REL_MAX_DIST] == REL_BUCKETS - 1
    return [int(n[bucket == b].max()) for b in range(REL_BUCKETS - 1)]


def _bias_tiles_kernel(tab_ref, o_ref, *, tile, upper):
    h = pl.program_id(0)
    key = lax.broadcasted_iota(jnp.int32, (tile, tile), 0)
    qry = lax.broadcasted_iota(jnp.int32, (tile, tile), 1)
    far = tab_ref[h, REL_BUCKETS - 1]
    for t in (BIAS_DIAGONAL, BIAS_PREVIOUS):
        dist = qry - key + t * tile
        val = jnp.zeros((tile, tile), F32)
        for b in range(REL_BUCKETS - 2, -1, -1):
            val = jnp.where(dist <= upper[b], (tab_ref[h, b] - far) * LOG2_E, val)
        o_ref[t] = jnp.where(dist >= 0, val, NEG_BIG)
    o_ref[BIAS_MASKED] = jnp.full((tile, tile), NEG_BIG, F32)


def _bias_tiles(rel_table_t, tile):
    assert tile >= REL_MAX_DIST
    return pl.pallas_call(
        functools.partial(_bias_tiles_kernel, tile=tile, upper=_bucket_upper_bounds()),
        grid=(DIFF_HEADS,),
        in_specs=[pl.BlockSpec(memory_space=pltpu.SMEM)],
        out_specs=pl.BlockSpec((None, 3, tile, tile), lambda h: (h, 0, 0, 0)),
        out_shape=jax.ShapeDtypeStruct((DIFF_HEADS, 3, tile, tile), F32),
        compiler_params=_params("parallel"),
        name="bias_tiles",
    )(rel_table_t)


def _diff_attn_kernel(qs_ref, k_ref, vt_ref, bias_ref, lam_ref, g_ref, o_ref,
                      s_ref, smax_ref, m_ref, acc_ref, *, tile, lambda_init):
    n_tiles = qs_ref.shape[0]
    assert ATTN_UNROLL % 2 == 0 and NEAR_UNROLL % 2 == 0 and n_tiles % NEAR_UNROLL == 0
    ones_rows = jnp.ones((SUM_ROWS, tile), BF16)
    last = n_tiles - 1

    def key_tile(j):
        return k_ref[pl.ds(pl.multiple_of(j * tile, tile), tile), :]

    def value_rows(j):
        return jnp.concatenate([vt_ref[j], ones_rows], axis=0)

    def put_scores(slot, s):
        s_ref[slot] = s
        smax_ref[slot] = jnp.max(s, axis=0, keepdims=True)

    def near_scores(i, base):
        i = jnp.minimum(i, last)
        qs = qs_ref[i]
        bias_t = bias_ref[BIAS_DIAGONAL]
        s = jnp.dot(key_tile(i), qs, preferred_element_type=F32)
        put_scores(base, s + jnp.concatenate([bias_t, bias_t], axis=1))
        bias_t = bias_ref[jnp.where(i >= 1, BIAS_PREVIOUS, BIAS_MASKED)]
        s = jnp.dot(key_tile(jnp.maximum(i - 1, 0)), qs, preferred_element_type=F32)
        put_scores(base + 1, s + jnp.concatenate([bias_t, bias_t], axis=1))

    def near_reduce(i, base):
        m_diag = smax_ref[base]
        p = jnp.exp2(s_ref[base] - m_diag).astype(BF16)
        acc = jnp.dot(value_rows(i), p, preferred_element_type=F32)
        m_new = jnp.maximum(m_diag, smax_ref[base + 1])
        p = jnp.exp2(s_ref[base + 1] - m_new).astype(BF16)
        acc = jnp.exp2(m_diag - m_new) * acc + jnp.dot(
            value_rows(jnp.maximum(i - 1, 0)), p, preferred_element_type=F32)
        acc_ref[i] = acc
        m_ref[i] = m_new

    def near_steps(g, carry):
        for u in range(NEAR_UNROLL):
            i = NEAR_UNROLL * g + u
            near_scores(i + 1, 2 * ((u + 1) % 2))
            near_reduce(i, 2 * (u % 2))
        return carry

    near_scores(jnp.int32(0), 0)
    lax.fori_loop(0, n_tiles // NEAR_UNROLL, near_steps, 0)

    def successor(i, j):
        wraps = j == i - 2
        return jnp.where(wraps, i + 1, i), jnp.where(wraps, 0, j + 1)

    def far_scores(i, j, slot):
        i = jnp.minimum(i, last)
        put_scores(slot, jnp.dot(key_tile(j), qs_ref[i], preferred_element_type=F32))

    def far_reduce(i, j, slot):
        m_prev = m_ref[i]
        m_new = jnp.maximum(m_prev, smax_ref[slot])
        p = jnp.exp2(s_ref[slot] - m_new).astype(BF16)
        acc_ref[i] = jnp.exp2(m_prev - m_new) * acc_ref[i] + jnp.dot(
            value_rows(j), p, preferred_element_type=F32)
        m_ref[i] = m_new

    lam = lam_ref[...]
    lam_full = (jnp.exp(jnp.sum(lam[0:1] * lam[1:2], axis=1, keepdims=True))
                - jnp.exp(jnp.sum(lam[2:3] * lam[3:4], axis=1, keepdims=True))
                + lambda_init)

    def emit(i):
        acc = acc_ref[i]
        inv_l = 1.0 / acc[DIFF_DV:DIFF_DV + 1]
        o_t = (acc[:DIFF_DV, :tile] * inv_l[:, :tile]
               - acc[:DIFF_DV, tile:] * (lam_full * inv_l[:, tile:]))
        scale = lax.rsqrt(jnp.mean(o_t * o_t, axis=0, keepdims=True) + RMS_EPS) * (1.0 - lambda_init)
        o_ref[i] = (o_t * scale * g_ref[...]).astype(o_ref.dtype)

    def far_steps(_, carry):
        i, j = carry
        emit(i - 1)
        for u in range(ATTN_UNROLL):
            i_next, j_next = successor(i, j)
            far_scores(i_next, j_next, (u + 1) % 2)
            far_reduce(i, j, u % 2)
            i, j = i_next, j_next
        return i, j

    far_pairs = [(i, j) for i in range(2, n_tiles) for j in range(i - 1)]
    n_loop = len(far_pairs) // ATTN_UNROLL * ATTN_UNROLL
    emitted = {far_pairs[t][0] - 1 for t in range(0, n_loop, ATTN_UNROLL)}
    first = jnp.int32(2), jnp.int32(0)
    far_scores(*first, 0)
    i, j = lax.fori_loop(0, n_loop // ATTN_UNROLL, far_steps, first)
    for u in range(n_loop, len(far_pairs)):
        i_next, j_next = successor(i, j)
        if u + 1 < len(far_pairs):
            far_scores(i_next, j_next, (u + 1) % 2)
        far_reduce(i, j, u % 2)
        i, j = i_next, j_next
    for i in range(n_tiles):
        if i not in emitted:
            emit(i)


def _diff_attention(k, qs_t, v_t, bias, lam, subln, lambda_init):
    b, s, _ = k.shape
    tile = ATTN_TILE
    nk = s // tile
    return pl.pallas_call(
        functools.partial(_diff_attn_kernel, tile=tile, lambda_init=lambda_init),
        grid=(b, DIFF_HEADS),
        in_specs=[pl.BlockSpec((None, nk, 2 * DIFF_DH, 2 * tile), lambda bi, h: (bi, 0, h, 0)),
                  pl.BlockSpec((None, s, 2 * DIFF_DH), lambda bi, h: (bi, 0, h)),
                  pl.BlockSpec((None, nk, DIFF_DV, tile), lambda bi, h: (bi, 0, h, 0)),
                  pl.BlockSpec((None, 3, tile, tile), lambda bi, h: (h, 0, 0, 0)),
                  pl.BlockSpec((4, DIFF_DH), lambda bi, h: (0, 0)),
                  pl.BlockSpec((DIFF_DV, 1), lambda bi, h: (0, 0))],
        out_specs=pl.BlockSpec((None, nk, DIFF_DV, tile), lambda bi, h: (bi, 0, h, 0)),
        out_shape=jax.ShapeDtypeStruct((b, nk, DIFF_HEADS * DIFF_DV, tile), BF16),
        scratch_shapes=[pltpu.VMEM((4, tile, 2 * tile), F32),
                        pltpu.VMEM((4, 1, 2 * tile), F32),
                        pltpu.VMEM((nk, 1, 2 * tile), F32),
                        pltpu.VMEM((nk, DIFF_DV + SUM_ROWS, 2 * tile), F32)],
        compiler_params=_params("parallel", "parallel"),
        name="diff_attention",
    )(qs_t, k, v_t, bias, lam, subln)


def kernel(x, norm_gains, ret_w_in, ret_w_out, diff_w_in, diff_w_out, diff_lambda,
           diff_subln, rel_bias_table, ffn_w_gate, ffn_w_up, ffn_w_down):
    b, s, d = x.shape
    t = b * s
    x2d = x.reshape(t, d)
    gains = norm_gains.reshape(DEPTH, 4, 1, d)

    half = RET_DK // 2
    inv = ROPE_BASE ** (-jnp.arange(half, dtype=F32) / half)
    ang = jnp.arange(s, dtype=F32)[:, None] * inv[None, :]
    cos, sin = jnp.cos(ang), jnp.sin(ang)
    log_gamma = jnp.log1p(-(2.0 ** (-5.0 - jnp.arange(RET_HEADS, dtype=F32))))

    bias = _bias_tiles(rel_bias_table.T, ATTN_TILE)

    assert 2 * DIFF_HEADS * DIFF_DH == DIFF_HEADS * DIFF_DV

    def f32_weights(layer):
        mixer = layer // N_MIXERS
        w_in, w_out = (ret_w_in, ret_w_out) if layer % N_MIXERS == 0 else (diff_w_in, diff_w_out)
        return [(w_in, mixer), (w_out, mixer), (ffn_w_gate, layer), (ffn_w_up, layer),
                (ffn_w_down, layer)]

    weights = [w[index].astype(BF16) for w, index in f32_weights(0)]

    for i in range(DEPTH):
        g = gains[i]
        w_in, w_out, w_gate, w_up, w_down = weights
        if i % N_MIXERS == 0:
            q, k, v, gate = _ret_in_proj(x2d, g[0], w_in, cos, sin)
            y = _retention(q, k, v, gate, log_gamma, b)
        else:
            lambda_init = 0.8 - 0.6 * math.exp(-0.3 * i)
            j = i // N_MIXERS
            k, qs_t, v_t = _diff_in_proj(x2d, g[0], w_in)
            nk = s // ATTN_TILE
            y = _diff_attention(k.reshape(b, s, -1),
                                qs_t.reshape(b, nk, -1, 2 * ATTN_TILE),
                                v_t.reshape(b, nk, -1, ATTN_TILE),
                                bias, diff_lambda[j], diff_subln[j].reshape(DIFF_DV, 1),
                                lambda_init)
        x2d, weights = _post_mixer(y, w_out, x2d, g[1], g[2], w_gate, w_up, w_down, g[3],
                                   to_cast=f32_weights(i + 1) if i + 1 < DEPTH else ())
    return x2d.reshape(b, s, d)
```

```python
import functools
import math

import numpy as np
import jax
import jax.numpy as jnp
from jax import lax
from jax.experimental import pallas as pl
from jax.experimental.pallas import tpu as pltpu

D_MODEL = 1024
DEPTH = 4
N_MIXERS = 2
RMS_EPS = 1e-6

RET_HEADS = 4
RET_DK = D_MODEL // RET_HEADS
RET_DV = 2 * RET_DK
RET_IN = 2 * RET_HEADS * RET_DK + 2 * RET_HEADS * RET_DV
RET_CHUNK = 256
ROPE_BASE = 10000.0

DIFF_HEADS = 8
DIFF_DH = D_MODEL // (2 * DIFF_HEADS)
DIFF_DV = 2 * DIFF_DH
NEG_BIG = -1e30

REL_BUCKETS = 32
REL_MAX_EXACT = REL_BUCKETS // 2
REL_MAX_DIST = 128

D_FF = -(-8 * D_MODEL // (3 * 256)) * 256

VMEM_LIMIT_BYTES = 56 * 1024 * 1024

IN_PROJ_COL_TILE = 512
IN_PROJ_ROW_GROUPS = 2
POST_ROW_TILE = 512
POST_ROW_GROUPS = 2
BF16_SUBLANES = 16
FFN_COL_TILE = 256
RET_SEQ_TILE = 2048
ATTN_TILE = 512
IN_PROJ_ROW_TILE = ATTN_TILE
ATTN_UNROLL = 8
NEAR_UNROLL = 4
DIAG_PARTS = 4
BIAS_DIAGONAL, BIAS_PREVIOUS, BIAS_MASKED = 0, 1, 2
SUM_ROWS = 16
LOG2_E = math.log2(math.e)

F32 = jnp.float32
BF16 = jnp.bfloat16


def _params(*semantics):
    return pltpu.CompilerParams(dimension_semantics=semantics,
                                vmem_limit_bytes=VMEM_LIMIT_BYTES)


def _rms_unit(xf):
    return xf * lax.rsqrt(jnp.mean(xf * xf, axis=-1, keepdims=True) + RMS_EPS)


def _diff_in_proj_kernel(x_ref, g_ref, wq_ref, wk_ref, wv_ref, ok_ref, oq_ref, ov_ref, h_ref):
    tm = x_ref.shape[0]
    groups = [slice(g * tm // IN_PROJ_ROW_GROUPS, (g + 1) * tm // IN_PROJ_ROW_GROUPS)
              for g in range(IN_PROJ_ROW_GROUPS)]
    for rows in groups:
        h_ref[rows, :] = (_rms_unit(x_ref[rows, :]) * g_ref[...]).astype(BF16)
    feature_major = (((0,), (1,)), ((), ()))
    for rows in groups:
        h = h_ref[rows, :]
        for c in range(ok_ref.shape[1] // IN_PROJ_COL_TILE):
            cols = slice(c * IN_PROJ_COL_TILE, (c + 1) * IN_PROJ_COL_TILE)
            ok_ref[rows, cols] = jnp.dot(h, wk_ref[:, cols],
                                         preferred_element_type=F32).astype(ok_ref.dtype)
        ov_ref[:, rows] = lax.dot_general(wv_ref[...], h, feature_major,
                                          preferred_element_type=F32).astype(ov_ref.dtype)
        q_t = lax.dot_general(wq_ref[...], h, feature_major, preferred_element_type=F32)
        q_t = q_t * (DIFF_DH ** -0.5 * LOG2_E)
        feat = lax.broadcasted_iota(jnp.int32, q_t.shape, 0) % (2 * DIFF_DH)
        oq_ref[:, rows] = jnp.where(feat < DIFF_DH, q_t, 0.0).astype(oq_ref.dtype)
        oq_ref[:, slice(tm + rows.start, tm + rows.stop)] = jnp.where(
            feat >= DIFF_DH, q_t, 0.0).astype(oq_ref.dtype)


def _resident_weight(rows, cols, col_block=0):
    return pl.BlockSpec((rows, cols), lambda i: (0, col_block), pipeline_mode=pl.Buffered(1))


def _diff_in_proj(x2d, g, w):
    t, d = x2d.shape
    n = w.shape[1] // 3
    tm = IN_PROJ_ROW_TILE
    tiles = lambda i: (i, 0, 0)
    return pl.pallas_call(
        _diff_in_proj_kernel,
        grid=(t // tm,),
        in_specs=[pl.BlockSpec((tm, d), lambda i: (i, 0)),
                  pl.BlockSpec((1, d), lambda i: (0, 0)),
                  _resident_weight(d, n, 0),
                  _resident_weight(d, n, 1),
                  _resident_weight(d, n, 2)],
        out_specs=[pl.BlockSpec((tm, n), lambda i: (i, 0)),
                   pl.BlockSpec((None, n, 2 * tm), tiles),
                   pl.BlockSpec((None, n, tm), tiles)],
        out_shape=[jax.ShapeDtypeStruct((t, n), BF16),
                   jax.ShapeDtypeStruct((t // tm, n, 2 * tm), BF16),
                   jax.ShapeDtypeStruct((t // tm, n, tm), BF16)],
        scratch_shapes=[pltpu.VMEM((tm, d), BF16)],
        compiler_params=_params("parallel"),
        name="diff_in_proj",
    )(x2d, g, w, w, w)


def _post_mixer_kernel(y_ref, wo_ref, x_ref, gmix_ref, gpre_ref, wg_ref, wu_ref, wd_ref,
                       gpost_ref, *rest, y_feature_major, n_cast):
    cast_in, (o_ref, *cast_out), (h_ref, a_ref) = (rest[:n_cast], rest[n_cast:2 * n_cast + 1],
                                                   rest[2 * n_cast + 1:])
    for src, dst in zip(cast_in, cast_out):
        dst[...] = src[...].astype(dst.dtype)
    tm = o_ref.shape[0]
    groups = [slice(g * tm // POST_ROW_GROUPS, (g + 1) * tm // POST_ROW_GROUPS)
              for g in range(POST_ROW_GROUPS)]
    for rows in groups:
        if y_feature_major:
            m = lax.dot_general(y_ref[:, rows], wo_ref[...], (((0,), (0,)), ((), ())),
                                preferred_element_type=F32)
        else:
            m = jnp.dot(y_ref[rows, :], wo_ref[...], preferred_element_type=F32)
        x1 = x_ref[rows, :] + _rms_unit(m) * gmix_ref[...]
        o_ref[rows, :] = x1
        h_ref[rows, :] = (_rms_unit(x1) * gpre_ref[...]).astype(BF16)
    for rows in groups:
        for c in range(a_ref.shape[1] // FFN_COL_TILE):
            cols = slice(c * FFN_COL_TILE, (c + 1) * FFN_COL_TILE)
            h = h_ref[rows, :]
            gate = jnp.dot(h, wg_ref[:, cols], preferred_element_type=F32)
            up = jnp.dot(h, wu_ref[:, cols], preferred_element_type=F32)
            a_ref[rows, cols] = (gate * jax.nn.sigmoid(gate) * up).astype(BF16)
    for rows in groups:
        f = jnp.dot(a_ref[rows, :], wd_ref[...], preferred_element_type=F32)
        o_ref[rows, :] += _rms_unit(f) * gpost_ref[...]


def _post_mixer(y, w_out, x2d, g_mix, g_pre, w_gate, w_up, w_down, g_post, to_cast=()):
    t, d = x2d.shape
    k = w_out.shape[0]
    ff = w_gate.shape[1]
    tm = POST_ROW_TILE
    steps = t // tm
    row = lambda i: (i, 0)
    fixed = lambda i: (0, 0)
    cast_args = [w for w, _ in to_cast]
    cast_in_specs, cast_out_specs = [], []
    for w, layer in to_cast:
        n_blocks = max(n for n in range(1, steps + 1)
                       if w.shape[1] % n == 0 and (w.shape[1] // n) % BF16_SUBLANES == 0)
        block = (w.shape[1] // n_blocks, w.shape[2])
        cast_in_specs.append(pl.BlockSpec(
            (None,) + block, lambda i, layer=layer, n=n_blocks: (layer, jnp.minimum(i, n - 1), 0)))
        cast_out_specs.append(pl.BlockSpec(
            block, lambda i, n=n_blocks: (jnp.minimum(i, n - 1), 0)))
    cast_out_shapes = [jax.ShapeDtypeStruct(w.shape[1:], BF16) for w in cast_args]
    feature_major = y.ndim == 4
    if feature_major:
        tiles_per_seq = y.shape[1]
        assert y.shape[2:] == (k, tm)
        y_spec = pl.BlockSpec((None, None, k, tm),
                              lambda i: (i // tiles_per_seq, i % tiles_per_seq, 0, 0))
    else:
        y_spec = pl.BlockSpec((tm, k), row)
    out, *cast = pl.pallas_call(
        functools.partial(_post_mixer_kernel, y_feature_major=feature_major,
                          n_cast=len(to_cast)),
        grid=(steps,),
        in_specs=[y_spec,
                  _resident_weight(k, d),
                  pl.BlockSpec((tm, d), row),
                  pl.BlockSpec((1, d), fixed),
                  pl.BlockSpec((1, d), fixed),
                  _resident_weight(d, ff),
                  _resident_weight(d, ff),
                  _resident_weight(ff, d),
                  pl.BlockSpec((1, d), fixed)] + cast_in_specs,
        out_specs=[pl.BlockSpec((tm, d), row)] + cast_out_specs,
        out_shape=[jax.ShapeDtypeStruct((t, d), F32)] + cast_out_shapes,
        scratch_shapes=[pltpu.VMEM((tm, d), BF16), pltpu.VMEM((tm, ff), BF16)],
        compiler_params=_params("arbitrary"),
        name="post_mixer",
    )(y, w_out, x2d, g_mix, g_pre, w_gate, w_up, w_down, g_post, *cast_args)
    return out, cast


def _ret_in_proj_kernel(x_ref, g_ref, w_ref, cos_ref, sin_ref, q_ref, k_ref, v_ref, gate_ref,
                        h_ref):
    half = RET_DK // 2
    k_col, v_col, gate_col = (RET_HEADS * RET_DK, 2 * RET_HEADS * RET_DK,
                              2 * RET_HEADS * RET_DK + RET_HEADS * RET_DV)
    tm = x_ref.shape[0]
    groups = [slice(g * tm // IN_PROJ_ROW_GROUPS, (g + 1) * tm // IN_PROJ_ROW_GROUPS)
              for g in range(IN_PROJ_ROW_GROUPS)]
    for rows in groups:
        h_ref[rows, :] = (_rms_unit(x_ref[rows, :]) * g_ref[...]).astype(BF16)

    for rows in groups:
        cos, sin = cos_ref[rows, :], sin_ref[rows, :]

        def project(col, width):
            return jnp.dot(h_ref[rows, :], w_ref[:, col:col + width],
                           preferred_element_type=F32)

        def rotate(t):
            t1, t2 = t[:, :half], t[:, half:]
            return jnp.concatenate([t1 * cos - t2 * sin, t1 * sin + t2 * cos], axis=1)

        for h in range(RET_HEADS):
            q_ref[h, rows, :] = rotate(project(h * RET_DK, RET_DK)).astype(q_ref.dtype)
            k = rotate(project(k_col + h * RET_DK, RET_DK)) * (RET_DK ** -0.5)
            k_ref[h, rows, :] = k.astype(k_ref.dtype)
            v_ref[h, rows, :] = project(v_col + h * RET_DV, RET_DV).astype(v_ref.dtype)
            gate = project(gate_col + h * RET_DV, RET_DV)
            gate_ref[h, rows, :] = (gate * jax.nn.sigmoid(gate)).astype(gate_ref.dtype)


def _ret_in_proj(x2d, g, w, cos, sin):
    t, d = x2d.shape
    tm = IN_PROJ_ROW_TILE
    tiles_per_seq = cos.shape[0] // tm
    row = lambda i: (i, 0)
    pos = lambda i: (i % tiles_per_seq, 0)
    head_rows = lambda i: (0, i, 0)
    shapes = [(RET_HEADS, t, RET_DK)] * 2 + [(RET_HEADS, t, RET_DV)] * 2
    return pl.pallas_call(
        _ret_in_proj_kernel,
        grid=(t // tm,),
        in_specs=[pl.BlockSpec((tm, d), row),
                  pl.BlockSpec((1, d), lambda i: (0, 0)),
                  _resident_weight(d, RET_IN),
                  pl.BlockSpec((tm, RET_DK // 2), pos),
                  pl.BlockSpec((tm, RET_DK // 2), pos)],
        out_specs=[pl.BlockSpec((RET_HEADS, tm, shape[2]), head_rows) for shape in shapes],
        out_shape=[jax.ShapeDtypeStruct(shape, BF16) for shape in shapes],
        scratch_shapes=[pltpu.VMEM((tm, d), BF16)],
        compiler_params=_params("parallel"),
        name="ret_in_proj",
    )(x2d, g, w, cos, sin)


def _retention_kernel(lg_ref, q_ref, k_ref, v_ref, gate_ref, o_ref, state_ref, *, n_chunks):
    c_len = RET_CHUNK

    @pl.when(pl.program_id(2) == 0)
    def _():
        state_ref[...] = jnp.zeros_like(state_ref)

    lg = lg_ref[pl.program_id(1)]
    ri = lax.broadcasted_iota(jnp.int32, (c_len, c_len), 0)
    ci = lax.broadcasted_iota(jnp.int32, (c_len, c_len), 1)
    diff = (ri - ci).astype(F32)
    decay_mask = jnp.where(diff >= 0, jnp.exp(jnp.maximum(diff, 0.0) * lg), 0.0)
    idx = lax.broadcasted_iota(jnp.int32, (c_len, 1), 0).astype(F32)
    q_decay = jnp.exp((idx + 1.0) * lg)
    k_decay = jnp.exp((c_len - 1.0 - idx) * lg)
    chunk_decay = jnp.exp(jnp.full((1, 1), c_len, F32) * lg)

    def chunk(c, carry):
        rows = pl.ds(pl.multiple_of(c * c_len, c_len), c_len)
        qb = q_ref[rows, :]
        kb = k_ref[rows, :]
        v = v_ref[rows, :]
        inner = lax.dot_general(qb, kb, (((1,), (1,)), ((), ())),
                                preferred_element_type=F32) * decay_mask
        out = jnp.dot(inner.astype(BF16), v, preferred_element_type=F32)
        state = state_ref[...]
        out = out + jnp.dot(qb, state.astype(BF16), preferred_element_type=F32) * q_decay
        kd_t = jnp.transpose(kb.astype(F32) * k_decay).astype(BF16)
        state_ref[...] = state * chunk_decay + jnp.dot(kd_t, v, preferred_element_type=F32)
        o_ref[rows, :] = (_rms_unit(out) * gate_ref[rows, :].astype(F32)).astype(o_ref.dtype)
        return carry

    lax.fori_loop(0, n_chunks, chunk, 0, unroll=True)


def _retention(q, k, v, gate, log_gamma, batch):
    t = q.shape[1]
    s = t // batch
    ts = RET_SEQ_TILE
    tiles_per_seq = s // ts
    head_rows = lambda bi, h, si: (h, bi * tiles_per_seq + si, 0)
    return pl.pallas_call(
        functools.partial(_retention_kernel, n_chunks=ts // RET_CHUNK),
        grid=(batch, RET_HEADS, tiles_per_seq),
        in_specs=[pl.BlockSpec(memory_space=pltpu.SMEM),
                  pl.BlockSpec((None, ts, RET_DK), head_rows),
                  pl.BlockSpec((None, ts, RET_DK), head_rows),
                  pl.BlockSpec((None, ts, RET_DV), head_rows),
                  pl.BlockSpec((None, ts, RET_DV), head_rows)],
        out_specs=pl.BlockSpec((ts, RET_DV), lambda bi, h, si: (bi * tiles_per_seq + si, h)),
        out_shape=jax.ShapeDtypeStruct((t, RET_HEADS * RET_DV), BF16),
        scratch_shapes=[pltpu.VMEM((RET_DK, RET_DV), F32)],
        compiler_params=_params("parallel", "parallel", "arbitrary"),
        name="retention",
    )(log_gamma, q, k, v, gate)


def _bucket_upper_bounds():
    n = np.arange(0, 4 * REL_MAX_DIST, dtype=np.int64)
    nf = np.maximum(n, 1).astype(np.float32)
    ratio = np.log(nf / np.float32(REL_MAX_EXACT)) / np.float32(
        math.log(REL_MAX_DIST / REL_MAX_EXACT)) * np.float32(REL_BUCKETS - REL_MAX_EXACT)
    large = np.minimum(REL_MAX_EXACT + ratio.astype(np.int32), REL_BUCKETS - 1)
    bucket = np.where(n < REL_MAX_EXACT, n, large)
    assert np.all(np.diff(bucket) >= 0) and bucket[REL_MAX_DIST] == REL_BUCKETS - 1
    return [int(n[bucket == b].max()) for b in range(REL_BUCKETS - 1)]


def _bias_tiles_kernel(tab_ref, o_ref, *, tile, upper):
    h = pl.program_id(0)
    key = lax.broadcasted_iota(jnp.int32, (tile, tile), 0)
    qry = lax.broadcasted_iota(jnp.int32, (tile, tile), 1)
    far = tab_ref[h, REL_BUCKETS - 1]
    for t in (BIAS_DIAGONAL, BIAS_PREVIOUS):
        dist = qry - key + t * tile
        val = jnp.zeros((tile, tile), F32)
        for b in range(REL_BUCKETS - 2, -1, -1):
            val = jnp.where(dist <= upper[b], (tab_ref[h, b] - far) * LOG2_E, val)
        o_ref[t] = jnp.where(dist >= 0, val, NEG_BIG)
    o_ref[BIAS_MASKED] = jnp.full((tile, tile), NEG_BIG, F32)


def _bias_tiles(rel_table_t, tile):
    assert tile >= REL_MAX_DIST
    return pl.pallas_call(
        functools.partial(_bias_tiles_kernel, tile=tile, upper=_bucket_upper_bounds()),
        grid=(DIFF_HEADS,),
        in_specs=[pl.BlockSpec(memory_space=pltpu.SMEM)],
        out_specs=pl.BlockSpec((None, 3, tile, tile), lambda h: (h, 0, 0, 0)),
        out_shape=jax.ShapeDtypeStruct((DIFF_HEADS, 3, tile, tile), F32),
        compiler_params=_params("parallel"),
        name="bias_tiles",
    )(rel_table_t)


def _diff_attn_kernel(qs_ref, k_ref, vt_ref, bias_ref, lam_ref, g_ref, o_ref,
                      s_ref, smax_ref, m_ref, acc_ref, *, tile, lambda_init):
    n_tiles = qs_ref.shape[0]
    assert ATTN_UNROLL % 2 == 0 and NEAR_UNROLL % 2 == 0 and n_tiles % NEAR_UNROLL == 0
    ones_rows = jnp.ones((SUM_ROWS, tile), BF16)
    last = n_tiles - 1

    def key_tile(j):
        return k_ref[pl.ds(pl.multiple_of(j * tile, tile), tile), :]

    def value_rows(j):
        return jnp.concatenate([vt_ref[j], ones_rows], axis=0)

    def put_scores(slot, s):
        s_ref[slot] = s
        smax_ref[slot] = jnp.max(s, axis=0, keepdims=True)

    band = tile // DIAG_PARTS

    def visible(a, r):
        return jnp.concatenate([a[:, r * band:tile], a[:, tile + r * band:]], axis=1)

    def full_width(a, r, fill):
        if r == 0:
            return a
        width = tile - r * band
        pad = jnp.full((a.shape[0], r * band), fill, a.dtype)
        return jnp.concatenate([pad, a[:, :width], pad, a[:, width:]], axis=1)

    def near_scores(i, base):
        i = jnp.minimum(i, last)
        qs = qs_ref[i]
        key_row = pl.multiple_of(i * tile, tile)
        m_diag = None
        for r in range(DIAG_PARTS):
            rows = slice(r * band, (r + 1) * band)
            bias_t = bias_ref[BIAS_DIAGONAL, rows, r * band:]
            s = jnp.dot(k_ref[pl.ds(key_row + r * band, band), :], visible(qs, r),
                        preferred_element_type=F32)
            s = s + jnp.concatenate([bias_t, bias_t], axis=1)
            s_ref[base, rows, :s.shape[1]] = s
            m_band = full_width(jnp.max(s, axis=0, keepdims=True), r, NEG_BIG)
            m_diag = m_band if m_diag is None else jnp.maximum(m_diag, m_band)
        smax_ref[base] = m_diag
        bias_t = bias_ref[jnp.where(i >= 1, BIAS_PREVIOUS, BIAS_MASKED)]
        s = jnp.dot(key_tile(jnp.maximum(i - 1, 0)), qs, preferred_element_type=F32)
        put_scores(base + 1, s + jnp.concatenate([bias_t, bias_t], axis=1))

    def near_reduce(i, base):
        m_diag = smax_ref[base]
        values = value_rows(i)
        acc = None
        for r in range(DIAG_PARTS):
            rows = slice(r * band, (r + 1) * band)
            width = 2 * (tile - r * band)
            p = jnp.exp2(s_ref[base, rows, :width] - visible(m_diag, r)).astype(BF16)
            part = full_width(jnp.dot(values[:, rows], p, preferred_element_type=F32), r, 0.0)
            acc = part if acc is None else acc + part
        m_new = jnp.maximum(m_diag, smax_ref[base + 1])
        p = jnp.exp2(s_ref[base + 1] - m_new).astype(BF16)
        acc = jnp.exp2(m_diag - m_new) * acc + jnp.dot(
            value_rows(jnp.maximum(i - 1, 0)), p, preferred_element_type=F32)
        acc_ref[i] = acc
        m_ref[i] = m_new

    def near_steps(g, carry):
        for u in range(NEAR_UNROLL):
            i = NEAR_UNROLL * g + u
            near_scores(i + 1, 2 * ((u + 1) % 2))
            near_reduce(i, 2 * (u % 2))
        return carry

    near_scores(jnp.int32(0), 0)
    lax.fori_loop(0, n_tiles // NEAR_UNROLL, near_steps, 0)

    def successor(i, j):
        wraps = j == i - 2
        return jnp.where(wraps, i + 1, i), jnp.where(wraps, 0, j + 1)

    def far_scores(i, j, slot):
        i = jnp.minimum(i, last)
        put_scores(slot, jnp.dot(key_tile(j), qs_ref[i], preferred_element_type=F32))

    def far_reduce(i, j, slot):
        m_prev = m_ref[i]
        m_new = jnp.maximum(m_prev, smax_ref[slot])
        p = jnp.exp2(s_ref[slot] - m_new).astype(BF16)
        acc_ref[i] = jnp.exp2(m_prev - m_new) * acc_ref[i] + jnp.dot(
            value_rows(j), p, preferred_element_type=F32)
        m_ref[i] = m_new

    lam = lam_ref[...]
    lam_full = (jnp.exp(jnp.sum(lam[0:1] * lam[1:2], axis=1, keepdims=True))
                - jnp.exp(jnp.sum(lam[2:3] * lam[3:4], axis=1, keepdims=True))
                + lambda_init)

    def emit(i):
        acc = acc_ref[i]
        inv_l = 1.0 / acc[DIFF_DV:DIFF_DV + 1]
        o_t = (acc[:DIFF_DV, :tile] * inv_l[:, :tile]
               - acc[:DIFF_DV, tile:] * (lam_full * inv_l[:, tile:]))
        scale = lax.rsqrt(jnp.mean(o_t * o_t, axis=0, keepdims=True) + RMS_EPS) * (1.0 - lambda_init)
        o_ref[i] = (o_t * scale * g_ref[...]).astype(o_ref.dtype)

    def far_steps(_, carry):
        i, j = carry
        emit(i - 1)
        for u in range(ATTN_UNROLL):
            i_next, j_next = successor(i, j)
            far_scores(i_next, j_next, (u + 1) % 2)
            far_reduce(i, j, u % 2)
            i, j = i_next, j_next
        return i, j

    far_pairs = [(i, j) for i in range(2, n_tiles) for j in range(i - 1)]
    n_loop = len(far_pairs) // ATTN_UNROLL * ATTN_UNROLL
    emitted = {far_pairs[t][0] - 1 for t in range(0, n_loop, ATTN_UNROLL)}
    first = jnp.int32(2), jnp.int32(0)
    far_scores(*first, 0)
    i, j = lax.fori_loop(0, n_loop // ATTN_UNROLL, far_steps, first)
    for u in range(n_loop, len(far_pairs)):
        i_next, j_next = successor(i, j)
        if u + 1 < len(far_pairs):
            far_scores(i_next, j_next, (u + 1) % 2)
        far_reduce(i, j, u % 2)
        i, j = i_next, j_next
    for i in range(n_tiles):
        if i not in emitted:
            emit(i)


def _diff_attention(k, qs_t, v_t, bias, lam, subln, lambda_init):
    b, s, _ = k.shape
    tile = ATTN_TILE
    nk = s // tile
    return pl.pallas_call(
        functools.partial(_diff_attn_kernel, tile=tile, lambda_init=lambda_init),
        grid=(b, DIFF_HEADS),
        in_specs=[pl.BlockSpec((None, nk, 2 * DIFF_DH, 2 * tile), lambda bi, h: (bi, 0, h, 0)),
                  pl.BlockSpec((None, s, 2 * DIFF_DH), lambda bi, h: (bi, 0, h)),
                  pl.BlockSpec((None, nk, DIFF_DV, tile), lambda bi, h: (bi, 0, h, 0)),
                  pl.BlockSpec((None, 3, tile, tile), lambda bi, h: (h, 0, 0, 0)),
                  pl.BlockSpec((4, DIFF_DH), lambda bi, h: (0, 0)),
                  pl.BlockSpec((DIFF_DV, 1), lambda bi, h: (0, 0))],
        out_specs=pl.BlockSpec((None, nk, DIFF_DV, tile), lambda bi, h: (bi, 0, h, 0)),
        out_shape=jax.ShapeDtypeStruct((b, nk, DIFF_HEADS * DIFF_DV, tile), BF16),
        scratch_shapes=[pltpu.VMEM((4, tile, 2 * tile), F32),
                        pltpu.VMEM((4, 1, 2 * tile), F32),
                        pltpu.VMEM((nk, 1, 2 * tile), F32),
                        pltpu.VMEM((nk, DIFF_DV + SUM_ROWS, 2 * tile), F32)],
        compiler_params=_params("parallel", "parallel"),
        name="diff_attention",
    )(qs_t, k, v_t, bias, lam, subln)


def kernel(x, norm_gains, ret_w_in, ret_w_out, diff_w_in, diff_w_out, diff_lambda,
           diff_subln, rel_bias_table, ffn_w_gate, ffn_w_up, ffn_w_down):
    b, s, d = x.shape
    t = b * s
    x2d = x.reshape(t, d)
    gains = norm_gains.reshape(DEPTH, 4, 1, d)

    half = RET_DK // 2
    inv = ROPE_BASE ** (-jnp.arange(half, dtype=F32) / half)
    ang = jnp.arange(s, dtype=F32)[:, None] * inv[None, :]
    cos, sin = jnp.cos(ang), jnp.sin(ang)
    log_gamma = jnp.log1p(-(2.0 ** (-5.0 - jnp.arange(RET_HEADS, dtype=F32))))

    bias = _bias_tiles(rel_bias_table.T, ATTN_TILE)

    assert 2 * DIFF_HEADS * DIFF_DH == DIFF_HEADS * DIFF_DV

    def f32_weights(layer):
        mixer = layer // N_MIXERS
        w_in, w_out = (ret_w_in, ret_w_out) if layer % N_MIXERS == 0 else (diff_w_in, diff_w_out)
        return [(w_in, mixer), (w_out, mixer), (ffn_w_gate, layer), (ffn_w_up, layer),
                (ffn_w_down, layer)]

    weights = [w[index].astype(BF16) for w, index in f32_weights(0)]

    for i in range(DEPTH):
        g = gains[i]
        w_in, w_out, w_gate, w_up, w_down = weights
        if i % N_MIXERS == 0:
            q, k, v, gate = _ret_in_proj(x2d, g[0], w_in, cos, sin)
            y = _retention(q, k, v, gate, log_gamma, b)
        else:
            lambda_init = 0.8 - 0.6 * math.exp(-0.3 * i)
            j = i // N_MIXERS
            k, qs_t, v_t = _diff_in_proj(x2d, g[0], w_in)
            nk = s // ATTN_TILE
            y = _diff_attention(k.reshape(b, s, -1),
                                qs_t.reshape(b, nk, -1, 2 * ATTN_TILE),
                                v_t.reshape(b, nk, -1, ATTN_TILE),
                                bias, diff_lambda[j], diff_subln[j].reshape(DIFF_DV, 1),
                                lambda_init)
        x2d, weights = _post_mixer(y, w_out, x2d, g[1], g[2], w_gate, w_up, w_down, g[3],
                                   to_cast=f32_weights(i + 1) if i + 1 < DEPTH else ())
    return x2d.reshape(b, s, d)
```

```python
import functools
import math

import numpy as np
import jax
import jax.numpy as jnp
from jax import lax
from jax.experimental import pallas as pl
from jax.experimental.pallas import tpu as pltpu

D_MODEL = 1024
DEPTH = 4
N_MIXERS = 2
RMS_EPS = 1e-6

RET_HEADS = 4
RET_DK = D_MODEL // RET_HEADS
RET_DV = 2 * RET_DK
RET_IN = 2 * RET_HEADS * RET_DK + 2 * RET_HEADS * RET_DV
RET_CHUNK = 256
ROPE_BASE = 10000.0

DIFF_HEADS = 8
DIFF_DH = D_MODEL // (2 * DIFF_HEADS)
DIFF_DV = 2 * DIFF_DH
NEG_BIG = -1e30

REL_BUCKETS = 32
REL_MAX_EXACT = REL_BUCKETS // 2
REL_MAX_DIST = 128

D_FF = -(-8 * D_MODEL // (3 * 256)) * 256

VMEM_LIMIT_BYTES = 56 * 1024 * 1024

IN_PROJ_COL_TILE = 512
IN_PROJ_ROW_GROUPS = 2
POST_ROW_TILE = 512
POST_ROW_GROUPS = 2
BF16_SUBLANES = 16
FFN_COL_TILE = 256
RET_SEQ_TILE = 2048
ATTN_TILE = 512
IN_PROJ_ROW_TILE = ATTN_TILE
ATTN_UNROLL = 8
NEAR_UNROLL = 4
DIAG_PARTS = 4
BIAS_DIAGONAL, BIAS_PREVIOUS, BIAS_MASKED = 0, 1, 2
SUM_ROWS = 16
LOG2_E = math.log2(math.e)

F32 = jnp.float32
BF16 = jnp.bfloat16


def _params(*semantics):
    return pltpu.CompilerParams(dimension_semantics=semantics,
                                vmem_limit_bytes=VMEM_LIMIT_BYTES)


def _rms_unit(xf):
    return xf * lax.rsqrt(jnp.mean(xf * xf, axis=-1, keepdims=True) + RMS_EPS)


def _diff_in_proj_kernel(x_ref, g_ref, wq_ref, wk_ref, wv_ref, ok_ref, oq_ref, ov_ref, h_ref):
    tm = x_ref.shape[0]
    groups = [slice(g * tm // IN_PROJ_ROW_GROUPS, (g + 1) * tm // IN_PROJ_ROW_GROUPS)
              for g in range(IN_PROJ_ROW_GROUPS)]
    for rows in groups:
        h_ref[rows, :] = (_rms_unit(x_ref[rows, :]) * g_ref[...]).astype(BF16)
    feature_major = (((0,), (1,)), ((), ()))
    for rows in groups:
        h = h_ref[rows, :]
        for c in range(ok_ref.shape[1] // IN_PROJ_COL_TILE):
            cols = slice(c * IN_PROJ_COL_TILE, (c + 1) * IN_PROJ_COL_TILE)
            ok_ref[rows, cols] = jnp.dot(h, wk_ref[:, cols],
                                         preferred_element_type=F32).astype(ok_ref.dtype)
        ov_ref[:, rows] = lax.dot_general(wv_ref[...], h, feature_major,
                                          preferred_element_type=F32).astype(ov_ref.dtype)
        q_t = lax.dot_general(wq_ref[...], h, feature_major, preferred_element_type=F32)
        q_t = q_t * (DIFF_DH ** -0.5 * LOG2_E)
        feat = lax.broadcasted_iota(jnp.int32, q_t.shape, 0) % (2 * DIFF_DH)
        oq_ref[:, rows] = jnp.where(feat < DIFF_DH, q_t, 0.0).astype(oq_ref.dtype)
        oq_ref[:, slice(tm + rows.start, tm + rows.stop)] = jnp.where(
            feat >= DIFF_DH, q_t, 0.0).astype(oq_ref.dtype)


def _resident_weight(rows, cols, col_block=0):
    return pl.BlockSpec((rows, cols), lambda i: (0, col_block), pipeline_mode=pl.Buffered(1))


def _diff_in_proj(x2d, g, w):
    t, d = x2d.shape
    n = w.shape[1] // 3
    tm = IN_PROJ_ROW_TILE
    tiles = lambda i: (i, 0, 0)
    return pl.pallas_call(
        _diff_in_proj_kernel,
        grid=(t // tm,),
        in_specs=[pl.BlockSpec((tm, d), lambda i: (i, 0)),
                  pl.BlockSpec((1, d), lambda i: (0, 0)),
                  _resident_weight(d, n, 0),
                  _resident_weight(d, n, 1),
                  _resident_weight(d, n, 2)],
        out_specs=[pl.BlockSpec((tm, n), lambda i: (i, 0)),
                   pl.BlockSpec((None, n, 2 * tm), tiles),
                   pl.BlockSpec((None, n, tm), tiles)],
        out_shape=[jax.ShapeDtypeStruct((t, n), BF16),
                   jax.ShapeDtypeStruct((t // tm, n, 2 * tm), BF16),
                   jax.ShapeDtypeStruct((t // tm, n, tm), BF16)],
        scratch_shapes=[pltpu.VMEM((tm, d), BF16)],
        compiler_params=_params("parallel"),
        name="diff_in_proj",
    )(x2d, g, w, w, w)


def _post_mixer_kernel(y_ref, wo_ref, x_ref, gmix_ref, gpre_ref, wg_ref, wu_ref, wd_ref,
                       gpost_ref, *rest, y_feature_major, n_cast):
    cast_in, (o_ref, *cast_out), (h_ref, a_ref) = (rest[:n_cast], rest[n_cast:2 * n_cast + 1],
                                                   rest[2 * n_cast + 1:])
    for src, dst in zip(cast_in, cast_out):
        dst[...] = src[...].astype(dst.dtype)
    tm = o_ref.shape[0]
    groups = [slice(g * tm // POST_ROW_GROUPS, (g + 1) * tm // POST_ROW_GROUPS)
              for g in range(POST_ROW_GROUPS)]
    for rows in groups:
        if y_feature_major:
            m = lax.dot_general(y_ref[:, rows], wo_ref[...], (((0,), (0,)), ((), ())),
                                preferred_element_type=F32)
        else:
            m = jnp.dot(y_ref[rows, :], wo_ref[...], preferred_element_type=F32)
        x1 = x_ref[rows, :] + _rms_unit(m) * gmix_ref[...]
        o_ref[rows, :] = x1
        h_ref[rows, :] = (_rms_unit(x1) * gpre_ref[...]).astype(BF16)
    for rows in groups:
        for c in range(a_ref.shape[1] // FFN_COL_TILE):
            cols = slice(c * FFN_COL_TILE, (c + 1) * FFN_COL_TILE)
            h = h_ref[rows, :]
            gate = jnp.dot(h, wg_ref[:, cols], preferred_element_type=F32)
            up = jnp.dot(h, wu_ref[:, cols], preferred_element_type=F32)
            a_ref[rows, cols] = (gate * jax.nn.sigmoid(gate) * up).astype(BF16)
    for rows in groups:
        f = jnp.dot(a_ref[rows, :], wd_ref[...], preferred_element_type=F32)
        o_ref[rows, :] += _rms_unit(f) * gpost_ref[...]


def _post_mixer(y, w_out, x2d, g_mix, g_pre, w_gate, w_up, w_down, g_post, to_cast=()):
    t, d = x2d.shape
    k = w_out.shape[0]
    ff = w_gate.shape[1]
    tm = POST_ROW_TILE
    steps = t // tm
    row = lambda i: (i, 0)
    fixed = lambda i: (0, 0)
    cast_args = [w for w, _ in to_cast]
    cast_in_specs, cast_out_specs = [], []
    for w, layer in to_cast:
        n_blocks = max(n for n in range(1, steps + 1)
                       if w.shape[1] % n == 0 and (w.shape[1] // n) % BF16_SUBLANES == 0)
        block = (w.shape[1] // n_blocks, w.shape[2])
        cast_in_specs.append(pl.BlockSpec(
            (None,) + block, lambda i, layer=layer, n=n_blocks: (layer, jnp.minimum(i, n - 1), 0)))
        cast_out_specs.append(pl.BlockSpec(
            block, lambda i, n=n_blocks: (jnp.minimum(i, n - 1), 0)))
    cast_out_shapes = [jax.ShapeDtypeStruct(w.shape[1:], BF16) for w in cast_args]
    feature_major = y.ndim == 4
    if feature_major:
        tiles_per_seq = y.shape[1]
        assert y.shape[2:] == (k, tm)
        y_spec = pl.BlockSpec((None, None, k, tm),
                              lambda i: (i // tiles_per_seq, i % tiles_per_seq, 0, 0))
    else:
        y_spec = pl.BlockSpec((tm, k), row)
    out, *cast = pl.pallas_call(
        functools.partial(_post_mixer_kernel, y_feature_major=feature_major,
                          n_cast=len(to_cast)),
        grid=(steps,),
        in_specs=[y_spec,
                  _resident_weight(k, d),
                  pl.BlockSpec((tm, d), row),
                  pl.BlockSpec((1, d), fixed),
                  pl.BlockSpec((1, d), fixed),
                  _resident_weight(d, ff),
                  _resident_weight(d, ff),
                  _resident_weight(ff, d),
                  pl.BlockSpec((1, d), fixed)] + cast_in_specs,
        out_specs=[pl.BlockSpec((tm, d), row)] + cast_out_specs,
        out_shape=[jax.ShapeDtypeStruct((t, d), F32)] + cast_out_shapes,
        scratch_shapes=[pltpu.VMEM((tm, d), BF16), pltpu.VMEM((tm, ff), BF16)],
        compiler_params=_params("arbitrary"),
        name="post_mixer",
    )(y, w_out, x2d, g_mix, g_pre, w_gate, w_up, w_down, g_post, *cast_args)
    return out, cast


def _ret_in_proj_kernel(x_ref, g_ref, w_ref, cos_ref, sin_ref, q_ref, k_ref, v_ref, gate_ref,
                        h_ref):
    half = RET_DK // 2
    k_col, v_col, gate_col = (RET_HEADS * RET_DK, 2 * RET_HEADS * RET_DK,
                              2 * RET_HEADS * RET_DK + RET_HEADS * RET_DV)
    tm = x_ref.shape[0]
    groups = [slice(g * tm // IN_PROJ_ROW_GROUPS, (g + 1) * tm // IN_PROJ_ROW_GROUPS)
              for g in range(IN_PROJ_ROW_GROUPS)]
    for rows in groups:
        h_ref[rows, :] = (_rms_unit(x_ref[rows, :]) * g_ref[...]).astype(BF16)

    for rows in groups:
        cos, sin = cos_ref[rows, :], sin_ref[rows, :]

        def project(col, width):
            return jnp.dot(h_ref[rows, :], w_ref[:, col:col + width],
                           preferred_element_type=F32)

        def rotate(t):
            t1, t2 = t[:, :half], t[:, half:]
            return jnp.concatenate([t1 * cos - t2 * sin, t1 * sin + t2 * cos], axis=1)

        for h in range(RET_HEADS):
            q_ref[h, rows, :] = rotate(project(h * RET_DK, RET_DK)).astype(q_ref.dtype)
            k = rotate(project(k_col + h * RET_DK, RET_DK)) * (RET_DK ** -0.5)
            k_ref[h, rows, :] = k.astype(k_ref.dtype)
            v_ref[h, rows, :] = project(v_col + h * RET_DV, RET_DV).astype(v_ref.dtype)
            gate = project(gate_col + h * RET_DV, RET_DV)
            gate_ref[h, rows, :] = (gate * jax.nn.sigmoid(gate)).astype(gate_ref.dtype)


def _ret_in_proj(x2d, g, w, cos, sin):
    t, d = x2d.shape
    tm = IN_PROJ_ROW_TILE
    tiles_per_seq = cos.shape[0] // tm
    row = lambda i: (i, 0)
    pos = lambda i: (i % tiles_per_seq, 0)
    head_rows = lambda i: (0, i, 0)
    shapes = [(RET_HEADS, t, RET_DK)] * 2 + [(RET_HEADS, t, RET_DV)] * 2
    return pl.pallas_call(
        _ret_in_proj_kernel,
        grid=(t // tm,),
        in_specs=[pl.BlockSpec((tm, d), row),
                  pl.BlockSpec((1, d), lambda i: (0, 0)),
                  _resident_weight(d, RET_IN),
                  pl.BlockSpec((tm, RET_DK // 2), pos),
                  pl.BlockSpec((tm, RET_DK // 2), pos)],
        out_specs=[pl.BlockSpec((RET_HEADS, tm, shape[2]), head_rows) for shape in shapes],
        out_shape=[jax.ShapeDtypeStruct(shape, BF16) for shape in shapes],
        scratch_shapes=[pltpu.VMEM((tm, d), BF16)],
        compiler_params=_params("parallel"),
        name="ret_in_proj",
    )(x2d, g, w, cos, sin)


def _retention_kernel(lg_ref, q_ref, k_ref, v_ref, gate_ref, o_ref, state_ref, *, n_chunks):
    c_len = RET_CHUNK

    @pl.when(pl.program_id(2) == 0)
    def _():
        state_ref[...] = jnp.zeros_like(state_ref)

    lg = lg_ref[pl.program_id(1)]
    ri = lax.broadcasted_iota(jnp.int32, (c_len, c_len), 0)
    ci = lax.broadcasted_iota(jnp.int32, (c_len, c_len), 1)
    diff = (ri - ci).astype(F32)
    decay_mask = jnp.where(diff >= 0, jnp.exp(jnp.maximum(diff, 0.0) * lg), 0.0)
    idx = lax.broadcasted_iota(jnp.int32, (c_len, 1), 0).astype(F32)
    q_decay = jnp.exp((idx + 1.0) * lg)
    k_decay = jnp.exp((c_len - 1.0 - idx) * lg)
    chunk_decay = jnp.exp(jnp.full((1, 1), c_len, F32) * lg)

    def chunk(c, carry):
        rows = pl.ds(pl.multiple_of(c * c_len, c_len), c_len)
        qb = q_ref[rows, :]
        kb = k_ref[rows, :]
        v = v_ref[rows, :]
        inner = (lax.dot_general(qb, kb, (((1,), (1,)), ((), ())),
                                 preferred_element_type=F32) * decay_mask).astype(BF16)
        kd_t = jnp.transpose(kb.astype(F32) * k_decay).astype(BF16)
        half = RET_DV // 2
        outs = []
        for cols in (slice(0, half), slice(half, RET_DV)):
            state = state_ref[:, cols]
            out = jnp.dot(inner, v[:, cols], preferred_element_type=F32)
            outs.append(out + jnp.dot(qb, state.astype(BF16),
                                      preferred_element_type=F32) * q_decay)
            state_ref[:, cols] = state * chunk_decay + jnp.dot(kd_t, v[:, cols],
                                                               preferred_element_type=F32)
        mean_sq = sum(jnp.sum(o * o, axis=-1, keepdims=True) for o in outs) / RET_DV
        inv_rms = lax.rsqrt(mean_sq + RMS_EPS)
        for cols, out in zip((slice(0, half), slice(half, RET_DV)), outs):
            gate = gate_ref[rows, cols].astype(F32)
            o_ref[rows, cols] = (out * inv_rms * gate).astype(o_ref.dtype)
        return carry

    lax.fori_loop(0, n_chunks, chunk, 0, unroll=True)


def _retention(q, k, v, gate, log_gamma, batch):
    t = q.shape[1]
    s = t // batch
    ts = RET_SEQ_TILE
    tiles_per_seq = s // ts
    head_rows = lambda bi, h, si: (h, bi * tiles_per_seq + si, 0)
    return pl.pallas_call(
        functools.partial(_retention_kernel, n_chunks=ts // RET_CHUNK),
        grid=(batch, RET_HEADS, tiles_per_seq),
        in_specs=[pl.BlockSpec(memory_space=pltpu.SMEM),
                  pl.BlockSpec((None, ts, RET_DK), head_rows),
                  pl.BlockSpec((None, ts, RET_DK), head_rows),
                  pl.BlockSpec((None, ts, RET_DV), head_rows),
                  pl.BlockSpec((None, ts, RET_DV), head_rows)],
        out_specs=pl.BlockSpec((ts, RET_DV), lambda bi, h, si: (bi * tiles_per_seq + si, h)),
        out_shape=jax.ShapeDtypeStruct((t, RET_HEADS * RET_DV), BF16),
        scratch_shapes=[pltpu.VMEM((RET_DK, RET_DV), F32)],
        compiler_params=_params("parallel", "parallel", "arbitrary"),
        name="retention",
    )(log_gamma, q, k, v, gate)


def _bucket_upper_bounds():
    n = np.arange(0, 4 * REL_MAX_DIST, dtype=np.int64)
    nf = np.maximum(n, 1).astype(np.float32)
    ratio = np.log(nf / np.float32(REL_MAX_EXACT)) / np.float32(
        math.log(REL_MAX_DIST / REL_MAX_EXACT)) * np.float32(REL_BUCKETS - REL_MAX_EXACT)
    large = np.minimum(REL_MAX_EXACT + ratio.astype(np.int32), REL_BUCKETS - 1)
    bucket = np.where(n < REL_MAX_EXACT, n, large)
    assert np.all(np.diff(bucket) >= 0) and bucket[REL_MAX_DIST] == REL_BUCKETS - 1
    return [int(n[bucket == b].max()) for b in range(REL_BUCKETS - 1)]


def _bias_tiles_kernel(tab_ref, o_ref, *, tile, upper):
    h = pl.program_id(0)
    key = lax.broadcasted_iota(jnp.int32, (tile, tile), 0)
    qry = lax.broadcasted_iota(jnp.int32, (tile, tile), 1)
    far = tab_ref[h, REL_BUCKETS - 1]
    for t in (BIAS_DIAGONAL, BIAS_PREVIOUS):
        dist = qry - key + t * tile
        val = jnp.zeros((tile, tile), F32)
        for b in range(REL_BUCKETS - 2, -1, -1):
            val = jnp.where(dist <= upper[b], (tab_ref[h, b] - far) * LOG2_E, val)
        o_ref[t] = jnp.where(dist >= 0, val, NEG_BIG)
    o_ref[BIAS_MASKED] = jnp.full((tile, tile), NEG_BIG, F32)


def _bias_tiles(rel_table_t, tile):
    assert tile >= REL_MAX_DIST
    return pl.pallas_call(
        functools.partial(_bias_tiles_kernel, tile=tile, upper=_bucket_upper_bounds()),
        grid=(DIFF_HEADS,),
        in_specs=[pl.BlockSpec(memory_space=pltpu.SMEM)],
        out_specs=pl.BlockSpec((None, 3, tile, tile), lambda h: (h, 0, 0, 0)),
        out_shape=jax.ShapeDtypeStruct((DIFF_HEADS, 3, tile, tile), F32),
        compiler_params=_params("parallel"),
        name="bias_tiles",
    )(rel_table_t)


def _diff_attn_kernel(qs_ref, k_ref, vt_ref, bias_ref, lam_ref, g_ref, o_ref,
                      s_ref, smax_ref, m_ref, acc_ref, *, tile, lambda_init):
    n_tiles = qs_ref.shape[0]
    assert ATTN_UNROLL % 2 == 0 and NEAR_UNROLL % 2 == 0 and n_tiles % NEAR_UNROLL == 0
    ones_rows = jnp.ones((SUM_ROWS, tile), BF16)
    last = n_tiles - 1

    def key_tile(j):
        return k_ref[pl.ds(pl.multiple_of(j * tile, tile), tile), :]

    def value_rows(j):
        return jnp.concatenate([vt_ref[j], ones_rows], axis=0)

    def put_scores(slot, s):
        s_ref[slot] = s
        smax_ref[slot] = jnp.max(s, axis=0, keepdims=True)

    band = tile // DIAG_PARTS

    def visible(a, r):
        return jnp.concatenate([a[:, r * band:tile], a[:, tile + r * band:]], axis=1)

    def full_width(a, r, fill):
        if r == 0:
            return a
        width = tile - r * band
        pad = jnp.full((a.shape[0], r * band), fill, a.dtype)
        return jnp.concatenate([pad, a[:, :width], pad, a[:, width:]], axis=1)

    def near_scores(i, base):
        i = jnp.minimum(i, last)
        qs = qs_ref[i]
        key_row = pl.multiple_of(i * tile, tile)
        m_diag = None
        for r in range(DIAG_PARTS):
            rows = slice(r * band, (r + 1) * band)
            bias_t = bias_ref[BIAS_DIAGONAL, rows, r * band:]
            s = jnp.dot(k_ref[pl.ds(key_row + r * band, band), :], visible(qs, r),
                        preferred_element_type=F32)
            s = s + jnp.concatenate([bias_t, bias_t], axis=1)
            s_ref[base, rows, :s.shape[1]] = s
            m_band = full_width(jnp.max(s, axis=0, keepdims=True), r, NEG_BIG)
            m_diag = m_band if m_diag is None else jnp.maximum(m_diag, m_band)
        smax_ref[base] = m_diag
        bias_t = bias_ref[jnp.where(i >= 1, BIAS_PREVIOUS, BIAS_MASKED)]
        s = jnp.dot(key_tile(jnp.maximum(i - 1, 0)), qs, preferred_element_type=F32)
        put_scores(base + 1, s + jnp.concatenate([bias_t, bias_t], axis=1))

    def near_reduce(i, base):
        m_diag = smax_ref[base]
        values = value_rows(i)
        acc = None
        for r in range(DIAG_PARTS):
            rows = slice(r * band, (r + 1) * band)
            width = 2 * (tile - r * band)
            p = jnp.exp2(s_ref[base, rows, :width] - visible(m_diag, r)).astype(BF16)
            part = full_width(jnp.dot(values[:, rows], p, preferred_element_type=F32), r, 0.0)
            acc = part if acc is None else acc + part
        m_new = jnp.maximum(m_diag, smax_ref[base + 1])
        p = jnp.exp2(s_ref[base + 1] - m_new).astype(BF16)
        acc = jnp.exp2(m_diag - m_new) * acc + jnp.dot(
            value_rows(jnp.maximum(i - 1, 0)), p, preferred_element_type=F32)
        acc_ref[i] = acc
        m_ref[i] = m_new

    def near_steps(g, carry):
        for u in range(NEAR_UNROLL):
            i = NEAR_UNROLL * g + u
            near_scores(i + 1, 2 * ((u + 1) % 2))
            near_reduce(i, 2 * (u % 2))
        return carry

    near_scores(jnp.int32(0), 0)
    lax.fori_loop(0, n_tiles // NEAR_UNROLL, near_steps, 0)

    def successor(i, j):
        wraps = j == i - 2
        return jnp.where(wraps, i + 1, i), jnp.where(wraps, 0, j + 1)

    def far_scores(i, j, slot):
        i = jnp.minimum(i, last)
        put_scores(slot, jnp.dot(key_tile(j), qs_ref[i], preferred_element_type=F32))

    def far_reduce(i, j, slot):
        m_prev = m_ref[i]
        m_new = jnp.maximum(m_prev, smax_ref[slot])
        p = jnp.exp2(s_ref[slot] - m_new).astype(BF16)
        acc_ref[i] = jnp.exp2(m_prev - m_new) * acc_ref[i] + jnp.dot(
            value_rows(j), p, preferred_element_type=F32)
        m_ref[i] = m_new

    lam = lam_ref[...]
    lam_full = (jnp.exp(jnp.sum(lam[0:1] * lam[1:2], axis=1, keepdims=True))
                - jnp.exp(jnp.sum(lam[2:3] * lam[3:4], axis=1, keepdims=True))
                + lambda_init)

    def emit(i):
        acc = acc_ref[i]
        inv_l = 1.0 / acc[DIFF_DV:DIFF_DV + 1]
        o_t = (acc[:DIFF_DV, :tile] * inv_l[:, :tile]
               - acc[:DIFF_DV, tile:] * (lam_full * inv_l[:, tile:]))
        scale = lax.rsqrt(jnp.mean(o_t * o_t, axis=0, keepdims=True) + RMS_EPS) * (1.0 - lambda_init)
        o_ref[i] = (o_t * scale * g_ref[...]).astype(o_ref.dtype)

    def far_steps(_, carry):
        i, j = carry
        emit(i - 1)
        for u in range(ATTN_UNROLL):
            i_next, j_next = successor(i, j)
            far_scores(i_next, j_next, (u + 1) % 2)
            far_reduce(i, j, u % 2)
            i, j = i_next, j_next
        return i, j

    far_pairs = [(i, j) for i in range(2, n_tiles) for j in range(i - 1)]
    n_loop = len(far_pairs) // ATTN_UNROLL * ATTN_UNROLL
    emitted = {far_pairs[t][0] - 1 for t in range(0, n_loop, ATTN_UNROLL)}
    first = jnp.int32(2), jnp.int32(0)
    far_scores(*first, 0)
    i, j = lax.fori_loop(0, n_loop // ATTN_UNROLL, far_steps, first)
    for u in range(n_loop, len(far_pairs)):
        i_next, j_next = successor(i, j)
        if u + 1 < len(far_pairs):
            far_scores(i_next, j_next, (u + 1) % 2)
        far_reduce(i, j, u % 2)
        i, j = i_next, j_next
    for i in range(n_tiles):
        if i not in emitted:
            emit(i)


def _diff_attention(k, qs_t, v_t, bias, lam, subln, lambda_init):
    b, s, _ = k.shape
    tile = ATTN_TILE
    nk = s // tile
    return pl.pallas_call(
        functools.partial(_diff_attn_kernel, tile=tile, lambda_init=lambda_init),
        grid=(b, DIFF_HEADS),
        in_specs=[pl.BlockSpec((None, nk, 2 * DIFF_DH, 2 * tile), lambda bi, h: (bi, 0, h, 0)),
                  pl.BlockSpec((None, s, 2 * DIFF_DH), lambda bi, h: (bi, 0, h)),
                  pl.BlockSpec((None, nk, DIFF_DV, tile), lambda bi, h: (bi, 0, h, 0)),
                  pl.BlockSpec((None, 3, tile, tile), lambda bi, h: (h, 0, 0, 0)),
                  pl.BlockSpec((4, DIFF_DH), lambda bi, h: (0, 0)),
                  pl.BlockSpec((DIFF_DV, 1), lambda bi, h: (0, 0))],
        out_specs=pl.BlockSpec((None, nk, DIFF_DV, tile), lambda bi, h: (bi, 0, h, 0)),
        out_shape=jax.ShapeDtypeStruct((b, nk, DIFF_HEADS * DIFF_DV, tile), BF16),
        scratch_shapes=[pltpu.VMEM((4, tile, 2 * tile), F32),
                        pltpu.VMEM((4, 1, 2 * tile), F32),
                        pltpu.VMEM((nk, 1, 2 * tile), F32),
                        pltpu.VMEM((nk, DIFF_DV + SUM_ROWS, 2 * tile), F32)],
        compiler_params=_params("parallel", "parallel"),
        name="diff_attention",
    )(qs_t, k, v_t, bias, lam, subln)


def kernel(x, norm_gains, ret_w_in, ret_w_out, diff_w_in, diff_w_out, diff_lambda,
           diff_subln, rel_bias_table, ffn_w_gate, ffn_w_up, ffn_w_down):
    b, s, d = x.shape
    t = b * s
    x2d = x.reshape(t, d)
    gains = norm_gains.reshape(DEPTH, 4, 1, d)

    half = RET_DK // 2
    inv = ROPE_BASE ** (-jnp.arange(half, dtype=F32) / half)
    ang = jnp.arange(s, dtype=F32)[:, None] * inv[None, :]
    cos, sin = jnp.cos(ang), jnp.sin(ang)
    log_gamma = jnp.log1p(-(2.0 ** (-5.0 - jnp.arange(RET_HEADS, dtype=F32))))

    bias = _bias_tiles(rel_bias_table.T, ATTN_TILE)

    assert 2 * DIFF_HEADS * DIFF_DH == DIFF_HEADS * DIFF_DV

    def f32_weights(layer):
        mixer = layer // N_MIXERS
        w_in, w_out = (ret_w_in, ret_w_out) if layer % N_MIXERS == 0 else (diff_w_in, diff_w_out)
        return [(w_in, mixer), (w_out, mixer), (ffn_w_gate, layer), (ffn_w_up, layer),
                (ffn_w_down, layer)]

    weights = [w[index].astype(BF16) for w, index in f32_weights(0)]

    for i in range(DEPTH):
        g = gains[i]
        w_in, w_out, w_gate, w_up, w_down = weights
        if i % N_MIXERS == 0:
            q, k, v, gate = _ret_in_proj(x2d, g[0], w_in, cos, sin)
            y = _retention(q, k, v, gate, log_gamma, b)
        else:
            lambda_init = 0.8 - 0.6 * math.exp(-0.3 * i)
            j = i // N_MIXERS
            k, qs_t, v_t = _diff_in_proj(x2d, g[0], w_in)
            nk = s // ATTN_TILE
            y = _diff_attention(k.reshape(b, s, -1),
                                qs_t.reshape(b, nk, -1, 2 * ATTN_TILE),
                                v_t.reshape(b, nk, -1, ATTN_TILE),
                                bias, diff_lambda[j], diff_subln[j].reshape(DIFF_DV, 1),
                                lambda_init)
        x2d, weights = _post_mixer(y, w_out, x2d, g[1], g[2], w_gate, w_up, w_down, g[3],
                                   to_cast=f32_weights(i + 1) if i + 1 < DEPTH else ())
    return x2d.reshape(b, s, d)
```

```python
import functools
import math

import numpy as np
import jax
import jax.numpy as jnp
from jax import lax
from jax.experimental import pallas as pl
from jax.experimental.pallas import tpu as pltpu

D_MODEL = 1024
DEPTH = 4
N_MIXERS = 2
RMS_EPS = 1e-6

RET_HEADS = 4
RET_DK = D_MODEL // RET_HEADS
RET_DV = 2 * RET_DK
RET_IN = 2 * RET_HEADS * RET_DK + 2 * RET_HEADS * RET_DV
RET_CHUNK = 256
ROPE_BASE = 10000.0

DIFF_HEADS = 8
DIFF_DH = D_MODEL // (2 * DIFF_HEADS)
DIFF_DV = 2 * DIFF_DH
NEG_BIG = -1e30

REL_BUCKETS = 32
REL_MAX_EXACT = REL_BUCKETS // 2
REL_MAX_DIST = 128

D_FF = -(-8 * D_MODEL // (3 * 256)) * 256

VMEM_LIMIT_BYTES = 56 * 1024 * 1024

IN_PROJ_COL_TILE = 512
IN_PROJ_ROW_GROUPS = 2
POST_ROW_TILE = 512
POST_ROW_GROUPS = 2
BF16_SUBLANES = 16
FFN_COL_TILE = 256
RET_SEQ_TILE = 2048
ATTN_TILE = 512
IN_PROJ_ROW_TILE = ATTN_TILE
ATTN_UNROLL = 16
NEAR_UNROLL = 8
DIAG_PARTS = 4
BIAS_DIAGONAL, BIAS_PREVIOUS, BIAS_MASKED = 0, 1, 2
SUM_ROWS = 16
LOG2_E = math.log2(math.e)

F32 = jnp.float32
BF16 = jnp.bfloat16


def _params(*semantics):
    return pltpu.CompilerParams(dimension_semantics=semantics,
                                vmem_limit_bytes=VMEM_LIMIT_BYTES)


def _rms_unit(xf):
    return xf * lax.rsqrt(jnp.mean(xf * xf, axis=-1, keepdims=True) + RMS_EPS)


def _diff_in_proj_kernel(x_ref, g_ref, wq_ref, wk_ref, wv_ref, ok_ref, oq_ref, ov_ref, h_ref):
    tm = x_ref.shape[0]
    groups = [slice(g * tm // IN_PROJ_ROW_GROUPS, (g + 1) * tm // IN_PROJ_ROW_GROUPS)
              for g in range(IN_PROJ_ROW_GROUPS)]
    for rows in groups:
        h_ref[rows, :] = (_rms_unit(x_ref[rows, :]) * g_ref[...]).astype(BF16)
    feature_major = (((0,), (1,)), ((), ()))
    for rows in groups:
        h = h_ref[rows, :]
        for c in range(ok_ref.shape[1] // IN_PROJ_COL_TILE):
            cols = slice(c * IN_PROJ_COL_TILE, (c + 1) * IN_PROJ_COL_TILE)
            ok_ref[rows, cols] = jnp.dot(h, wk_ref[:, cols],
                                         preferred_element_type=F32).astype(ok_ref.dtype)
        ov_ref[:, rows] = lax.dot_general(wv_ref[...], h, feature_major,
                                          preferred_element_type=F32).astype(ov_ref.dtype)
        q_t = lax.dot_general(wq_ref[...], h, feature_major, preferred_element_type=F32)
        q_t = q_t * (DIFF_DH ** -0.5 * LOG2_E)
        feat = lax.broadcasted_iota(jnp.int32, q_t.shape, 0) % (2 * DIFF_DH)
        oq_ref[:, rows] = jnp.where(feat < DIFF_DH, q_t, 0.0).astype(oq_ref.dtype)
        oq_ref[:, slice(tm + rows.start, tm + rows.stop)] = jnp.where(
            feat >= DIFF_DH, q_t, 0.0).astype(oq_ref.dtype)


def _resident_weight(rows, cols, col_block=0):
    return pl.BlockSpec((rows, cols), lambda i: (0, col_block), pipeline_mode=pl.Buffered(1))


def _diff_in_proj(x2d, g, w):
    t, d = x2d.shape
    n = w.shape[1] // 3
    tm = IN_PROJ_ROW_TILE
    tiles = lambda i: (i, 0, 0)
    return pl.pallas_call(
        _diff_in_proj_kernel,
        grid=(t // tm,),
        in_specs=[pl.BlockSpec((tm, d), lambda i: (i, 0)),
                  pl.BlockSpec((1, d), lambda i: (0, 0)),
                  _resident_weight(d, n, 0),
                  _resident_weight(d, n, 1),
                  _resident_weight(d, n, 2)],
        out_specs=[pl.BlockSpec((tm, n), lambda i: (i, 0)),
                   pl.BlockSpec((None, n, 2 * tm), tiles),
                   pl.BlockSpec((None, n, tm), tiles)],
        out_shape=[jax.ShapeDtypeStruct((t, n), BF16),
                   jax.ShapeDtypeStruct((t // tm, n, 2 * tm), BF16),
                   jax.ShapeDtypeStruct((t // tm, n, tm), BF16)],
        scratch_shapes=[pltpu.VMEM((tm, d), BF16)],
        compiler_params=_params("parallel"),
        name="diff_in_proj",
    )(x2d, g, w, w, w)


def _post_mixer_kernel(y_ref, wo_ref, x_ref, gmix_ref, gpre_ref, wg_ref, wu_ref, wd_ref,
                       gpost_ref, *rest, y_feature_major, n_cast):
    cast_in, (o_ref, *cast_out), (h_ref, a_ref) = (rest[:n_cast], rest[n_cast:2 * n_cast + 1],
                                                   rest[2 * n_cast + 1:])
    for src, dst in zip(cast_in, cast_out):
        dst[...] = src[...].astype(dst.dtype)
    tm = o_ref.shape[0]
    groups = [slice(g * tm // POST_ROW_GROUPS, (g + 1) * tm // POST_ROW_GROUPS)
              for g in range(POST_ROW_GROUPS)]
    for rows in groups:
        if y_feature_major:
            m = lax.dot_general(y_ref[:, rows], wo_ref[...], (((0,), (0,)), ((), ())),
                                preferred_element_type=F32)
        else:
            m = jnp.dot(y_ref[rows, :], wo_ref[...], preferred_element_type=F32)
        x1 = x_ref[rows, :] + _rms_unit(m) * gmix_ref[...]
        o_ref[rows, :] = x1
        h_ref[rows, :] = (_rms_unit(x1) * gpre_ref[...]).astype(BF16)
    for rows in groups:
        for c in range(a_ref.shape[1] // FFN_COL_TILE):
            cols = slice(c * FFN_COL_TILE, (c + 1) * FFN_COL_TILE)
            h = h_ref[rows, :]
            gate = jnp.dot(h, wg_ref[:, cols], preferred_element_type=F32)
            up = jnp.dot(h, wu_ref[:, cols], preferred_element_type=F32)
            a_ref[rows, cols] = (gate * jax.nn.sigmoid(gate) * up).astype(BF16)
    for rows in groups:
        f = jnp.dot(a_ref[rows, :], wd_ref[...], preferred_element_type=F32)
        o_ref[rows, :] += _rms_unit(f) * gpost_ref[...]


def _post_mixer(y, w_out, x2d, g_mix, g_pre, w_gate, w_up, w_down, g_post, to_cast=()):
    t, d = x2d.shape
    k = w_out.shape[0]
    ff = w_gate.shape[1]
    tm = POST_ROW_TILE
    steps = t // tm
    row = lambda i: (i, 0)
    fixed = lambda i: (0, 0)
    cast_args = [w for w, _ in to_cast]
    cast_in_specs, cast_out_specs = [], []
    for w, layer in to_cast:
        n_blocks = max(n for n in range(1, steps + 1)
                       if w.shape[1] % n == 0 and (w.shape[1] // n) % BF16_SUBLANES == 0)
        block = (w.shape[1] // n_blocks, w.shape[2])
        cast_in_specs.append(pl.BlockSpec(
            (None,) + block, lambda i, layer=layer, n=n_blocks: (layer, jnp.minimum(i, n - 1), 0)))
        cast_out_specs.append(pl.BlockSpec(
            block, lambda i, n=n_blocks: (jnp.minimum(i, n - 1), 0)))
    cast_out_shapes = [jax.ShapeDtypeStruct(w.shape[1:], BF16) for w in cast_args]
    feature_major = y.ndim == 4
    if feature_major:
        tiles_per_seq = y.shape[1]
        assert y.shape[2:] == (k, tm)
        y_spec = pl.BlockSpec((None, None, k, tm),
                              lambda i: (i // tiles_per_seq, i % tiles_per_seq, 0, 0))
    else:
        y_spec = pl.BlockSpec((tm, k), row)
    out, *cast = pl.pallas_call(
        functools.partial(_post_mixer_kernel, y_feature_major=feature_major,
                          n_cast=len(to_cast)),
        grid=(steps,),
        in_specs=[y_spec,
                  _resident_weight(k, d),
                  pl.BlockSpec((tm, d), row),
                  pl.BlockSpec((1, d), fixed),
                  pl.BlockSpec((1, d), fixed),
                  _resident_weight(d, ff),
                  _resident_weight(d, ff),
                  _resident_weight(ff, d),
                  pl.BlockSpec((1, d), fixed)] + cast_in_specs,
        out_specs=[pl.BlockSpec((tm, d), row)] + cast_out_specs,
        out_shape=[jax.ShapeDtypeStruct((t, d), F32)] + cast_out_shapes,
        scratch_shapes=[pltpu.VMEM((tm, d), BF16), pltpu.VMEM((tm, ff), BF16)],
        compiler_params=_params("arbitrary"),
        name="post_mixer",
    )(y, w_out, x2d, g_mix, g_pre, w_gate, w_up, w_down, g_post, *cast_args)
    return out, cast


def _ret_in_proj_kernel(x_ref, g_ref, w_ref, cos_ref, sin_ref, q_ref, k_ref, v_ref, gate_ref,
                        h_ref):
    half = RET_DK // 2
    k_col, v_col, gate_col = (RET_HEADS * RET_DK, 2 * RET_HEADS * RET_DK,
                              2 * RET_HEADS * RET_DK + RET_HEADS * RET_DV)
    tm = x_ref.shape[0]
    groups = [slice(g * tm // IN_PROJ_ROW_GROUPS, (g + 1) * tm // IN_PROJ_ROW_GROUPS)
              for g in range(IN_PROJ_ROW_GROUPS)]
    for rows in groups:
        h_ref[rows, :] = (_rms_unit(x_ref[rows, :]) * g_ref[...]).astype(BF16)

    for rows in groups:
        cos, sin = cos_ref[rows, :], sin_ref[rows, :]

        def project(col, width):
            return jnp.dot(h_ref[rows, :], w_ref[:, col:col + width],
                           preferred_element_type=F32)

        def rotate(t):
            t1, t2 = t[:, :half], t[:, half:]
            return jnp.concatenate([t1 * cos - t2 * sin, t1 * sin + t2 * cos], axis=1)

        for h in range(RET_HEADS):
            q_ref[h, rows, :] = rotate(project(h * RET_DK, RET_DK)).astype(q_ref.dtype)
            k = rotate(project(k_col + h * RET_DK, RET_DK)) * (RET_DK ** -0.5)
            k_ref[h, rows, :] = k.astype(k_ref.dtype)
            v_ref[h, rows, :] = project(v_col + h * RET_DV, RET_DV).astype(v_ref.dtype)
            gate = project(gate_col + h * RET_DV, RET_DV)
            gate_ref[h, rows, :] = (gate * jax.nn.sigmoid(gate)).astype(gate_ref.dtype)


def _ret_in_proj(x2d, g, w, cos, sin):
    t, d = x2d.shape
    tm = IN_PROJ_ROW_TILE
    tiles_per_seq = cos.shape[0] // tm
    row = lambda i: (i, 0)
    pos = lambda i: (i % tiles_per_seq, 0)
    head_rows = lambda i: (0, i, 0)
    shapes = [(RET_HEADS, t, RET_DK)] * 2 + [(RET_HEADS, t, RET_DV)] * 2
    return pl.pallas_call(
        _ret_in_proj_kernel,
        grid=(t // tm,),
        in_specs=[pl.BlockSpec((tm, d), row),
                  pl.BlockSpec((1, d), lambda i: (0, 0)),
                  _resident_weight(d, RET_IN),
                  pl.BlockSpec((tm, RET_DK // 2), pos),
                  pl.BlockSpec((tm, RET_DK // 2), pos)],
        out_specs=[pl.BlockSpec((RET_HEADS, tm, shape[2]), head_rows) for shape in shapes],
        out_shape=[jax.ShapeDtypeStruct(shape, BF16) for shape in shapes],
        scratch_shapes=[pltpu.VMEM((tm, d), BF16)],
        compiler_params=_params("parallel"),
        name="ret_in_proj",
    )(x2d, g, w, cos, sin)


def _retention_kernel(lg_ref, q_ref, k_ref, v_ref, gate_ref, o_ref, state_ref, *, n_chunks):
    c_len = RET_CHUNK

    @pl.when(pl.program_id(2) == 0)
    def _():
        state_ref[...] = jnp.zeros_like(state_ref)

    lg = lg_ref[pl.program_id(1)]
    ri = lax.broadcasted_iota(jnp.int32, (c_len, c_len), 0)
    ci = lax.broadcasted_iota(jnp.int32, (c_len, c_len), 1)
    diff = (ri - ci).astype(F32)
    decay_mask = jnp.where(diff >= 0, jnp.exp(jnp.maximum(diff, 0.0) * lg), 0.0)
    idx = lax.broadcasted_iota(jnp.int32, (c_len, 1), 0).astype(F32)
    q_decay = jnp.exp((idx + 1.0) * lg)
    k_decay = jnp.exp((c_len - 1.0 - idx) * lg)
    chunk_decay = jnp.exp(jnp.full((1, 1), c_len, F32) * lg)

    def chunk(c, carry):
        rows = pl.ds(pl.multiple_of(c * c_len, c_len), c_len)
        qb = q_ref[rows, :]
        kb = k_ref[rows, :]
        v = v_ref[rows, :]
        inner = (lax.dot_general(qb, kb, (((1,), (1,)), ((), ())),
                                 preferred_element_type=F32) * decay_mask).astype(BF16)
        kd_t = jnp.transpose(kb.astype(F32) * k_decay).astype(BF16)
        half = RET_DV // 2
        outs = []
        for cols in (slice(0, half), slice(half, RET_DV)):
            state = state_ref[:, cols]
            out = jnp.dot(inner, v[:, cols], preferred_element_type=F32)
            outs.append(out + jnp.dot(qb, state.astype(BF16),
                                      preferred_element_type=F32) * q_decay)
            state_ref[:, cols] = state * chunk_decay + jnp.dot(kd_t, v[:, cols],
                                                               preferred_element_type=F32)
        mean_sq = sum(jnp.sum(o * o, axis=-1, keepdims=True) for o in outs) / RET_DV
        inv_rms = lax.rsqrt(mean_sq + RMS_EPS)
        for cols, out in zip((slice(0, half), slice(half, RET_DV)), outs):
            gate = gate_ref[rows, cols].astype(F32)
            o_ref[rows, cols] = (out * inv_rms * gate).astype(o_ref.dtype)
        return carry

    lax.fori_loop(0, n_chunks, chunk, 0, unroll=True)


def _retention(q, k, v, gate, log_gamma, batch):
    t = q.shape[1]
    s = t // batch
    ts = RET_SEQ_TILE
    tiles_per_seq = s // ts
    head_rows = lambda bi, h, si: (h, bi * tiles_per_seq + si, 0)
    return pl.pallas_call(
        functools.partial(_retention_kernel, n_chunks=ts // RET_CHUNK),
        grid=(batch, RET_HEADS, tiles_per_seq),
        in_specs=[pl.BlockSpec(memory_space=pltpu.SMEM),
                  pl.BlockSpec((None, ts, RET_DK), head_rows),
                  pl.BlockSpec((None, ts, RET_DK), head_rows),
                  pl.BlockSpec((None, ts, RET_DV), head_rows),
                  pl.BlockSpec((None, ts, RET_DV), head_rows)],
        out_specs=pl.BlockSpec((ts, RET_DV), lambda bi, h, si: (bi * tiles_per_seq + si, h)),
        out_shape=jax.ShapeDtypeStruct((t, RET_HEADS * RET_DV), BF16),
        scratch_shapes=[pltpu.VMEM((RET_DK, RET_DV), F32)],
        compiler_params=_params("parallel", "parallel", "arbitrary"),
        name="retention",
    )(log_gamma, q, k, v, gate)


def _bucket_upper_bounds():
    n = np.arange(0, 4 * REL_MAX_DIST, dtype=np.int64)
    nf = np.maximum(n, 1).astype(np.float32)
    ratio = np.log(nf / np.float32(REL_MAX_EXACT)) / np.float32(
        math.log(REL_MAX_DIST / REL_MAX_EXACT)) * np.float32(REL_BUCKETS - REL_MAX_EXACT)
    large = np.minimum(REL_MAX_EXACT + ratio.astype(np.int32), REL_BUCKETS - 1)
    bucket = np.where(n < REL_MAX_EXACT, n, large)
    assert np.all(np.diff(bucket) >= 0) and bucket[REL_MAX_DIST] == REL_BUCKETS - 1
    return [int(n[bucket == b].max()) for b in range(REL_BUCKETS - 1)]


def _bias_tiles_kernel(tab_ref, o_ref, *, tile, upper):
    h = pl.program_id(0)
    key = lax.broadcasted_iota(jnp.int32, (tile, tile), 0)
    qry = lax.broadcasted_iota(jnp.int32, (tile, tile), 1)
    far = tab_ref[h, REL_BUCKETS - 1]
    for t in (BIAS_DIAGONAL, BIAS_PREVIOUS):
        dist = qry - key + t * tile
        val = jnp.zeros((tile, tile), F32)
        for b in range(REL_BUCKETS - 2, -1, -1):
            val = jnp.where(dist <= upper[b], (tab_ref[h, b] - far) * LOG2_E, val)
        o_ref[t] = jnp.where(dist >= 0, val, NEG_BIG)
    o_ref[BIAS_MASKED] = jnp.full((tile, tile), NEG_BIG, F32)


def _bias_tiles(rel_table_t, tile):
    assert tile >= REL_MAX_DIST
    return pl.pallas_call(
        functools.partial(_bias_tiles_kernel, tile=tile, upper=_bucket_upper_bounds()),
        grid=(DIFF_HEADS,),
        in_specs=[pl.BlockSpec(memory_space=pltpu.SMEM)],
        out_specs=pl.BlockSpec((None, 3, tile, tile), lambda h: (h, 0, 0, 0)),
        out_shape=jax.ShapeDtypeStruct((DIFF_HEADS, 3, tile, tile), F32),
        compiler_params=_params("parallel"),
        name="bias_tiles",
    )(rel_table_t)


def _diff_attn_kernel(qs_ref, k_ref, vt_ref, bias_ref, lam_ref, g_ref, o_ref,
                      s_ref, smax_ref, m_ref, acc_ref, *, tile, lambda_init):
    n_tiles = qs_ref.shape[0]
    assert ATTN_UNROLL % 2 == 0 and NEAR_UNROLL % 2 == 0 and n_tiles % NEAR_UNROLL == 0
    ones_rows = jnp.ones((SUM_ROWS, tile), BF16)
    last = n_tiles - 1

    def key_tile(j):
        return k_ref[pl.ds(pl.multiple_of(j * tile, tile), tile), :]

    def value_rows(j):
        return jnp.concatenate([vt_ref[j], ones_rows], axis=0)

    def put_scores(slot, s):
        s_ref[slot] = s
        smax_ref[slot] = jnp.max(s, axis=0, keepdims=True)

    band = tile // DIAG_PARTS

    def visible(a, r):
        return jnp.concatenate([a[:, r * band:tile], a[:, tile + r * band:]], axis=1)

    def full_width(a, r, fill):
        if r == 0:
            return a
        width = tile - r * band
        pad = jnp.full((a.shape[0], r * band), fill, a.dtype)
        return jnp.concatenate([pad, a[:, :width], pad, a[:, width:]], axis=1)

    def near_scores(i, base):
        i = jnp.minimum(i, last)
        qs = qs_ref[i]
        key_row = pl.multiple_of(i * tile, tile)
        m_diag = None
        for r in range(DIAG_PARTS):
            rows = slice(r * band, (r + 1) * band)
            bias_t = bias_ref[BIAS_DIAGONAL, rows, r * band:]
            s = jnp.dot(k_ref[pl.ds(key_row + r * band, band), :], visible(qs, r),
                        preferred_element_type=F32)
            s = s + jnp.concatenate([bias_t, bias_t], axis=1)
            s_ref[base, rows, :s.shape[1]] = s
            m_band = full_width(jnp.max(s, axis=0, keepdims=True), r, NEG_BIG)
            m_diag = m_band if m_diag is None else jnp.maximum(m_diag, m_band)
        smax_ref[base] = m_diag
        near_rows = tile - REL_MAX_DIST
        bias_t = bias_ref[jnp.where(i >= 1, BIAS_PREVIOUS, BIAS_MASKED), near_rows:]
        s = jnp.dot(key_tile(jnp.maximum(i - 1, 0)), qs, preferred_element_type=F32)
        s = jnp.concatenate([s[:near_rows] + jnp.where(i >= 1, 0.0, NEG_BIG),
                             s[near_rows:] + jnp.concatenate([bias_t, bias_t], axis=1)], axis=0)
        put_scores(base + 1, s)

    def near_reduce(i, base):
        m_diag = smax_ref[base]
        values = value_rows(i)
        acc = None
        for r in range(DIAG_PARTS):
            rows = slice(r * band, (r + 1) * band)
            width = 2 * (tile - r * band)
            p = jnp.exp2(s_ref[base, rows, :width] - visible(m_diag, r)).astype(BF16)
            part = full_width(jnp.dot(values[:, rows], p, preferred_element_type=F32), r, 0.0)
            acc = part if acc is None else acc + part
        m_new = jnp.maximum(m_diag, smax_ref[base + 1])
        p = jnp.exp2(s_ref[base + 1] - m_new).astype(BF16)
        acc = jnp.exp2(m_diag - m_new) * acc + jnp.dot(
            value_rows(jnp.maximum(i - 1, 0)), p, preferred_element_type=F32)
        acc_ref[i] = acc
        m_ref[i] = m_new

    def near_steps(g, carry):
        for u in range(NEAR_UNROLL):
            i = NEAR_UNROLL * g + u
            near_scores(i + 1, 2 * ((u + 1) % 2))
            near_reduce(i, 2 * (u % 2))
        return carry

    near_scores(jnp.int32(0), 0)
    lax.fori_loop(0, n_tiles // NEAR_UNROLL, near_steps, 0)

    def successor(i, j):
        wraps = j == i - 2
        return jnp.where(wraps, i + 1, i), jnp.where(wraps, 0, j + 1)

    def far_scores(i, j, slot):
        i = jnp.minimum(i, last)
        put_scores(slot, jnp.dot(key_tile(j), qs_ref[i], preferred_element_type=F32))

    def far_reduce(i, j, slot):
        m_prev = m_ref[i]
        m_new = jnp.maximum(m_prev, smax_ref[slot])
        p = jnp.exp2(s_ref[slot] - m_new).astype(BF16)
        acc_ref[i] = jnp.exp2(m_prev - m_new) * acc_ref[i] + jnp.dot(
            value_rows(j), p, preferred_element_type=F32)
        m_ref[i] = m_new

    lam = lam_ref[...]
    lam_full = (jnp.exp(jnp.sum(lam[0:1] * lam[1:2], axis=1, keepdims=True))
                - jnp.exp(jnp.sum(lam[2:3] * lam[3:4], axis=1, keepdims=True))
                + lambda_init)

    def emit(i):
        acc = acc_ref[i]
        inv_l = 1.0 / acc[DIFF_DV:DIFF_DV + 1]
        o_t = (acc[:DIFF_DV, :tile] * inv_l[:, :tile]
               - acc[:DIFF_DV, tile:] * (lam_full * inv_l[:, tile:]))
        scale = lax.rsqrt(jnp.mean(o_t * o_t, axis=0, keepdims=True) + RMS_EPS) * (1.0 - lambda_init)
        o_ref[i] = (o_t * scale * g_ref[...]).astype(o_ref.dtype)

    def far_steps(_, carry):
        i, j = carry
        emit(i - 1)
        for u in range(ATTN_UNROLL):
            i_next, j_next = successor(i, j)
            far_scores(i_next, j_next, (u + 1) % 2)
            far_reduce(i, j, u % 2)
            i, j = i_next, j_next
        return i, j

    far_pairs = [(i, j) for i in range(2, n_tiles) for j in range(i - 1)]
    n_loop = len(far_pairs) // ATTN_UNROLL * ATTN_UNROLL
    emitted = {far_pairs[t][0] - 1 for t in range(0, n_loop, ATTN_UNROLL)}
    first = jnp.int32(2), jnp.int32(0)
    far_scores(*first, 0)
    i, j = lax.fori_loop(0, n_loop // ATTN_UNROLL, far_steps, first)
    for u in range(n_loop, len(far_pairs)):
        i_next, j_next = successor(i, j)
        if u + 1 < len(far_pairs):
            far_scores(i_next, j_next, (u + 1) % 2)
        far_reduce(i, j, u % 2)
        i, j = i_next, j_next
    for i in range(n_tiles):
        if i not in emitted:
            emit(i)


def _diff_attention(k, qs_t, v_t, bias, lam, subln, lambda_init):
    b, s, _ = k.shape
    tile = ATTN_TILE
    nk = s // tile
    return pl.pallas_call(
        functools.partial(_diff_attn_kernel, tile=tile, lambda_init=lambda_init),
        grid=(b, DIFF_HEADS),
        in_specs=[pl.BlockSpec((None, nk, 2 * DIFF_DH, 2 * tile), lambda bi, h: (bi, 0, h, 0)),
                  pl.BlockSpec((None, s, 2 * DIFF_DH), lambda bi, h: (bi, 0, h)),
                  pl.BlockSpec((None, nk, DIFF_DV, tile), lambda bi, h: (bi, 0, h, 0)),
                  pl.BlockSpec((None, 3, tile, tile), lambda bi, h: (h, 0, 0, 0)),
                  pl.BlockSpec((4, DIFF_DH), lambda bi, h: (0, 0)),
                  pl.BlockSpec((DIFF_DV, 1), lambda bi, h: (0, 0))],
        out_specs=pl.BlockSpec((None, nk, DIFF_DV, tile), lambda bi, h: (bi, 0, h, 0)),
        out_shape=jax.ShapeDtypeStruct((b, nk, DIFF_HEADS * DIFF_DV, tile), BF16),
        scratch_shapes=[pltpu.VMEM((4, tile, 2 * tile), F32),
                        pltpu.VMEM((4, 1, 2 * tile), F32),
                        pltpu.VMEM((nk, 1, 2 * tile), F32),
                        pltpu.VMEM((nk, DIFF_DV + SUM_ROWS, 2 * tile), F32)],
        compiler_params=_params("parallel", "parallel"),
        name="diff_attention",
    )(qs_t, k, v_t, bias, lam, subln)


def kernel(x, norm_gains, ret_w_in, ret_w_out, diff_w_in, diff_w_out, diff_lambda,
           diff_subln, rel_bias_table, ffn_w_gate, ffn_w_up, ffn_w_down):
    b, s, d = x.shape
    t = b * s
    x2d = x.reshape(t, d)
    gains = norm_gains.reshape(DEPTH, 4, 1, d)

    half = RET_DK // 2
    inv = ROPE_BASE ** (-jnp.arange(half, dtype=F32) / half)
    ang = jnp.arange(s, dtype=F32)[:, None] * inv[None, :]
    cos, sin = jnp.cos(ang), jnp.sin(ang)
    log_gamma = jnp.log1p(-(2.0 ** (-5.0 - jnp.arange(RET_HEADS, dtype=F32))))

    bias = _bias_tiles(rel_bias_table.T, ATTN_TILE)

    assert 2 * DIFF_HEADS * DIFF_DH == DIFF_HEADS * DIFF_DV

    def f32_weights(layer):
        mixer = layer // N_MIXERS
        w_in, w_out = (ret_w_in, ret_w_out) if layer % N_MIXERS == 0 else (diff_w_in, diff_w_out)
        return [(w_in, mixer), (w_out, mixer), (ffn_w_gate, layer), (ffn_w_up, layer),
                (ffn_w_down, layer)]

    weights = [w[index].astype(BF16) for w, index in f32_weights(0)]

    for i in range(DEPTH):
        g = gains[i]
        w_in, w_out, w_gate, w_up, w_down = weights
        if i % N_MIXERS == 0:
            q, k, v, gate = _ret_in_proj(x2d, g[0], w_in, cos, sin)
            y = _retention(q, k, v, gate, log_gamma, b)
        else:
            lambda_init = 0.8 - 0.6 * math.exp(-0.3 * i)
            j = i // N_MIXERS
            k, qs_t, v_t = _diff_in_proj(x2d, g[0], w_in)
            nk = s // ATTN_TILE
            y = _diff_attention(k.reshape(b, s, -1),
                                qs_t.reshape(b, nk, -1, 2 * ATTN_TILE),
                                v_t.reshape(b, nk, -1, ATTN_TILE),
                                bias, diff_lambda[j], diff_subln[j].reshape(DIFF_DV, 1),
                                lambda_init)
        x2d, weights = _post_mixer(y, w_out, x2d, g[1], g[2], w_gate, w_up, w_down, g[3],
                                   to_cast=f32_weights(i + 1) if i + 1 < DEPTH else ())
    return x2d.reshape(b, s, d)
```

```python
import functools
import math

import numpy as np
import jax
import jax.numpy as jnp
from jax import lax
from jax.experimental import pallas as pl
from jax.experimental.pallas import tpu as pltpu

D_MODEL = 1024
DEPTH = 4
N_MIXERS = 2
RMS_EPS = 1e-6

RET_HEADS = 4
RET_DK = D_MODEL // RET_HEADS
RET_DV = 2 * RET_DK
RET_IN = 2 * RET_HEADS * RET_DK + 2 * RET_HEADS * RET_DV
RET_CHUNK = 256
ROPE_BASE = 10000.0

DIFF_HEADS = 8
DIFF_DH = D_MODEL // (2 * DIFF_HEADS)
DIFF_DV = 2 * DIFF_DH
NEG_BIG = -1e30

REL_BUCKETS = 32
REL_MAX_EXACT = REL_BUCKETS // 2
REL_MAX_DIST = 128

D_FF = -(-8 * D_MODEL // (3 * 256)) * 256

VMEM_LIMIT_BYTES = 56 * 1024 * 1024

IN_PROJ_COL_TILE = 512
IN_PROJ_ROW_GROUPS = 2
POST_ROW_TILE = 512
POST_ROW_GROUPS = 2
BF16_SUBLANES = 16
FFN_COL_TILE = 256
RET_SEQ_TILE = 2048
RET_INPUT_BUFFERS = 3
ATTN_TILE = 512
IN_PROJ_ROW_TILE = ATTN_TILE
ATTN_UNROLL = 16
NEAR_UNROLL = 8
DIAG_PARTS = 4
BIAS_DIAGONAL, BIAS_PREVIOUS, BIAS_MASKED = 0, 1, 2
SUM_ROWS = 16
LOG2_E = math.log2(math.e)

F32 = jnp.float32
BF16 = jnp.bfloat16


def _params(*semantics):
    return pltpu.CompilerParams(dimension_semantics=semantics,
                                vmem_limit_bytes=VMEM_LIMIT_BYTES)


def _rms_unit(xf):
    return xf * lax.rsqrt(jnp.mean(xf * xf, axis=-1, keepdims=True) + RMS_EPS)


def _diff_in_proj_kernel(x_ref, g_ref, wq_ref, wk_ref, wv_ref, ok_ref, oq_ref, ov_ref, h_ref):
    tm = x_ref.shape[0]
    groups = [slice(g * tm // IN_PROJ_ROW_GROUPS, (g + 1) * tm // IN_PROJ_ROW_GROUPS)
              for g in range(IN_PROJ_ROW_GROUPS)]
    for rows in groups:
        h_ref[rows, :] = (_rms_unit(x_ref[rows, :]) * g_ref[...]).astype(BF16)
    feature_major = (((0,), (1,)), ((), ()))
    for rows in groups:
        h = h_ref[rows, :]
        for c in range(ok_ref.shape[1] // IN_PROJ_COL_TILE):
            cols = slice(c * IN_PROJ_COL_TILE, (c + 1) * IN_PROJ_COL_TILE)
            ok_ref[rows, cols] = jnp.dot(h, wk_ref[:, cols],
                                         preferred_element_type=F32).astype(ok_ref.dtype)
        ov_ref[:, rows] = lax.dot_general(wv_ref[...], h, feature_major,
                                          preferred_element_type=F32).astype(ov_ref.dtype)
        q_t = lax.dot_general(wq_ref[...], h, feature_major, preferred_element_type=F32)
        q_t = q_t * (DIFF_DH ** -0.5 * LOG2_E)
        feat = lax.broadcasted_iota(jnp.int32, q_t.shape, 0) % (2 * DIFF_DH)
        oq_ref[:, rows] = jnp.where(feat < DIFF_DH, q_t, 0.0).astype(oq_ref.dtype)
        oq_ref[:, slice(tm + rows.start, tm + rows.stop)] = jnp.where(
            feat >= DIFF_DH, q_t, 0.0).astype(oq_ref.dtype)


def _resident_weight(rows, cols, col_block=0):
    return pl.BlockSpec((rows, cols), lambda i: (0, col_block), pipeline_mode=pl.Buffered(1))


def _diff_in_proj(x2d, g, w):
    t, d = x2d.shape
    n = w.shape[1] // 3
    tm = IN_PROJ_ROW_TILE
    tiles = lambda i: (i, 0, 0)
    return pl.pallas_call(
        _diff_in_proj_kernel,
        grid=(t // tm,),
        in_specs=[pl.BlockSpec((tm, d), lambda i: (i, 0)),
                  pl.BlockSpec((1, d), lambda i: (0, 0)),
                  _resident_weight(d, n, 0),
                  _resident_weight(d, n, 1),
                  _resident_weight(d, n, 2)],
        out_specs=[pl.BlockSpec((tm, n), lambda i: (i, 0)),
                   pl.BlockSpec((None, n, 2 * tm), tiles),
                   pl.BlockSpec((None, n, tm), tiles)],
        out_shape=[jax.ShapeDtypeStruct((t, n), BF16),
                   jax.ShapeDtypeStruct((t // tm, n, 2 * tm), BF16),
                   jax.ShapeDtypeStruct((t // tm, n, tm), BF16)],
        scratch_shapes=[pltpu.VMEM((tm, d), BF16)],
        compiler_params=_params("parallel"),
        name="diff_in_proj",
    )(x2d, g, w, w, w)


def _post_mixer_kernel(y_ref, wo_ref, x_ref, gmix_ref, gpre_ref, wg_ref, wu_ref, wd_ref,
                       gpost_ref, *rest, y_feature_major, n_cast):
    cast_in, (o_ref, *cast_out), (h_ref, a_ref) = (rest[:n_cast], rest[n_cast:2 * n_cast + 1],
                                                   rest[2 * n_cast + 1:])
    for src, dst in zip(cast_in, cast_out):
        dst[...] = src[...].astype(dst.dtype)
    tm = o_ref.shape[0]
    groups = [slice(g * tm // POST_ROW_GROUPS, (g + 1) * tm // POST_ROW_GROUPS)
              for g in range(POST_ROW_GROUPS)]
    for rows in groups:
        if y_feature_major:
            m = lax.dot_general(y_ref[:, rows], wo_ref[...], (((0,), (0,)), ((), ())),
                                preferred_element_type=F32)
        else:
            m = jnp.dot(y_ref[rows, :], wo_ref[...], preferred_element_type=F32)
        x1 = x_ref[rows, :] + _rms_unit(m) * gmix_ref[...]
        o_ref[rows, :] = x1
        h_ref[rows, :] = (_rms_unit(x1) * gpre_ref[...]).astype(BF16)
    for rows in groups:
        for c in range(a_ref.shape[1] // FFN_COL_TILE):
            cols = slice(c * FFN_COL_TILE, (c + 1) * FFN_COL_TILE)
            h = h_ref[rows, :]
            gate = jnp.dot(h, wg_ref[:, cols], preferred_element_type=F32)
            up = jnp.dot(h, wu_ref[:, cols], preferred_element_type=F32)
            a_ref[rows, cols] = (gate * jax.nn.sigmoid(gate) * up).astype(BF16)
    for rows in groups:
        f = jnp.dot(a_ref[rows, :], wd_ref[...], preferred_element_type=F32)
        o_ref[rows, :] += _rms_unit(f) * gpost_ref[...]


def _post_mixer(y, w_out, x2d, g_mix, g_pre, w_gate, w_up, w_down, g_post, to_cast=()):
    t, d = x2d.shape
    k = w_out.shape[0]
    ff = w_gate.shape[1]
    tm = POST_ROW_TILE
    steps = t // tm
    row = lambda i: (i, 0)
    fixed = lambda i: (0, 0)
    cast_args = [w for w, _ in to_cast]
    cast_in_specs, cast_out_specs = [], []
    for w, layer in to_cast:
        n_blocks = max(n for n in range(1, steps + 1)
                       if w.shape[1] % n == 0 and (w.shape[1] // n) % BF16_SUBLANES == 0)
        block = (w.shape[1] // n_blocks, w.shape[2])
        cast_in_specs.append(pl.BlockSpec(
            (None,) + block, lambda i, layer=layer, n=n_blocks: (layer, jnp.minimum(i, n - 1), 0)))
        cast_out_specs.append(pl.BlockSpec(
            block, lambda i, n=n_blocks: (jnp.minimum(i, n - 1), 0)))
    cast_out_shapes = [jax.ShapeDtypeStruct(w.shape[1:], BF16) for w in cast_args]
    feature_major = y.ndim == 4
    if feature_major:
        tiles_per_seq = y.shape[1]
        assert y.shape[2:] == (k, tm)
        y_spec = pl.BlockSpec((None, None, k, tm),
                              lambda i: (i // tiles_per_seq, i % tiles_per_seq, 0, 0))
    else:
        y_spec = pl.BlockSpec((tm, k), row)
    out, *cast = pl.pallas_call(
        functools.partial(_post_mixer_kernel, y_feature_major=feature_major,
                          n_cast=len(to_cast)),
        grid=(steps,),
        in_specs=[y_spec,
                  _resident_weight(k, d),
                  pl.BlockSpec((tm, d), row),
                  pl.BlockSpec((1, d), fixed),
                  pl.BlockSpec((1, d), fixed),
                  _resident_weight(d, ff),
                  _resident_weight(d, ff),
                  _resident_weight(ff, d),
                  pl.BlockSpec((1, d), fixed)] + cast_in_specs,
        out_specs=[pl.BlockSpec((tm, d), row)] + cast_out_specs,
        out_shape=[jax.ShapeDtypeStruct((t, d), F32)] + cast_out_shapes,
        scratch_shapes=[pltpu.VMEM((tm, d), BF16), pltpu.VMEM((tm, ff), BF16)],
        compiler_params=_params("arbitrary"),
        name="post_mixer",
    )(y, w_out, x2d, g_mix, g_pre, w_gate, w_up, w_down, g_post, *cast_args)
    return out, cast


def _ret_in_proj_kernel(x_ref, g_ref, w_ref, cos_ref, sin_ref, q_ref, k_ref, v_ref, gate_ref,
                        h_ref):
    half = RET_DK // 2
    k_col, v_col, gate_col = (RET_HEADS * RET_DK, 2 * RET_HEADS * RET_DK,
                              2 * RET_HEADS * RET_DK + RET_HEADS * RET_DV)
    tm = x_ref.shape[0]
    groups = [slice(g * tm // IN_PROJ_ROW_GROUPS, (g + 1) * tm // IN_PROJ_ROW_GROUPS)
              for g in range(IN_PROJ_ROW_GROUPS)]
    for rows in groups:
        h_ref[rows, :] = (_rms_unit(x_ref[rows, :]) * g_ref[...]).astype(BF16)

    for rows in groups:
        cos, sin = cos_ref[rows, :], sin_ref[rows, :]

        def project(col, width):
            return jnp.dot(h_ref[rows, :], w_ref[:, col:col + width],
                           preferred_element_type=F32)

        def rotate(t):
            t1, t2 = t[:, :half], t[:, half:]
            return jnp.concatenate([t1 * cos - t2 * sin, t1 * sin + t2 * cos], axis=1)

        for h in range(RET_HEADS):
            q_ref[h, rows, :] = rotate(project(h * RET_DK, RET_DK)).astype(q_ref.dtype)
            k = rotate(project(k_col + h * RET_DK, RET_DK)) * (RET_DK ** -0.5)
            k_ref[h, rows, :] = k.astype(k_ref.dtype)
            v_ref[h, rows, :] = project(v_col + h * RET_DV, RET_DV).astype(v_ref.dtype)
            gate = project(gate_col + h * RET_DV, RET_DV)
            gate_ref[h, rows, :] = (gate * jax.nn.sigmoid(gate)).astype(gate_ref.dtype)


def _ret_in_proj(x2d, g, w, cos, sin):
    t, d = x2d.shape
    tm = IN_PROJ_ROW_TILE
    tiles_per_seq = cos.shape[0] // tm
    row = lambda i: (i, 0)
    pos = lambda i: (i % tiles_per_seq, 0)
    head_rows = lambda i: (0, i, 0)
    shapes = [(RET_HEADS, t, RET_DK)] * 2 + [(RET_HEADS, t, RET_DV)] * 2
    return pl.pallas_call(
        _ret_in_proj_kernel,
        grid=(t // tm,),
        in_specs=[pl.BlockSpec((tm, d), row),
                  pl.BlockSpec((1, d), lambda i: (0, 0)),
                  _resident_weight(d, RET_IN),
                  pl.BlockSpec((tm, RET_DK // 2), pos),
                  pl.BlockSpec((tm, RET_DK // 2), pos)],
        out_specs=[pl.BlockSpec((RET_HEADS, tm, shape[2]), head_rows) for shape in shapes],
        out_shape=[jax.ShapeDtypeStruct(shape, BF16) for shape in shapes],
        scratch_shapes=[pltpu.VMEM((tm, d), BF16)],
        compiler_params=_params("parallel"),
        name="ret_in_proj",
    )(x2d, g, w, cos, sin)


def _retention_kernel(lg_ref, q_hbm, k_hbm, v_hbm, gate_hbm, o_hbm, state_ref, step_ref, *,
                      n_chunks, batch, tiles_per_seq, ts):
    step_ref[0] = 0

    def step(q_ref, k_ref, v_ref, gate_ref, o_ref):
        n = step_ref[0]
        step_ref[0] = n + 1
        _retention_tile(lg_ref, q_ref, k_ref, v_ref, gate_ref, o_ref, state_ref,
                        head=(n // tiles_per_seq) % RET_HEADS, first=n % tiles_per_seq == 0,
                        n_chunks=n_chunks)

    deep = pl.Buffered(RET_INPUT_BUFFERS)
    head_rows = lambda bi, h, si: (h, bi * tiles_per_seq + si, 0)
    pltpu.emit_pipeline(
        step, grid=(batch, RET_HEADS, tiles_per_seq),
        in_specs=[pl.BlockSpec((None, ts, RET_DK), head_rows, pipeline_mode=deep),
                  pl.BlockSpec((None, ts, RET_DK), head_rows, pipeline_mode=deep),
                  pl.BlockSpec((None, ts, RET_DV), head_rows, pipeline_mode=deep),
                  pl.BlockSpec((None, ts, RET_DV), head_rows, pipeline_mode=deep)],
        out_specs=[pl.BlockSpec((ts, RET_DV), lambda bi, h, si: (bi * tiles_per_seq + si, h))],
    )(q_hbm, k_hbm, v_hbm, gate_hbm, o_hbm)


def _retention_tile(lg_ref, q_ref, k_ref, v_ref, gate_ref, o_ref, state_ref, *,
                    head, first, n_chunks):
    c_len = RET_CHUNK

    @pl.when(first)
    def _():
        state_ref[...] = jnp.zeros_like(state_ref)

    lg = lg_ref[head]
    ri = lax.broadcasted_iota(jnp.int32, (c_len, c_len), 0)
    ci = lax.broadcasted_iota(jnp.int32, (c_len, c_len), 1)
    diff = (ri - ci).astype(F32)
    decay_mask = jnp.where(diff >= 0, jnp.exp(jnp.maximum(diff, 0.0) * lg), 0.0)
    idx = lax.broadcasted_iota(jnp.int32, (c_len, 1), 0).astype(F32)
    q_decay = jnp.exp((idx + 1.0) * lg)
    k_decay = jnp.exp((c_len - 1.0 - idx) * lg)
    chunk_decay = jnp.exp(jnp.full((1, 1), c_len, F32) * lg)

    def chunk(c, carry):
        rows = pl.ds(pl.multiple_of(c * c_len, c_len), c_len)
        qb = q_ref[rows, :]
        kb = k_ref[rows, :]
        v = v_ref[rows, :]
        inner = (lax.dot_general(qb, kb, (((1,), (1,)), ((), ())),
                                 preferred_element_type=F32) * decay_mask).astype(BF16)
        kd_t = jnp.transpose(kb.astype(F32) * k_decay).astype(BF16)
        half = RET_DV // 2
        outs = []
        for cols in (slice(0, half), slice(half, RET_DV)):
            state = state_ref[:, cols]
            out = jnp.dot(inner, v[:, cols], preferred_element_type=F32)
            outs.append(out + jnp.dot(qb, state.astype(BF16),
                                      preferred_element_type=F32) * q_decay)
            state_ref[:, cols] = state * chunk_decay + jnp.dot(kd_t, v[:, cols],
                                                               preferred_element_type=F32)
        mean_sq = sum(jnp.sum(o * o, axis=-1, keepdims=True) for o in outs) / RET_DV
        inv_rms = lax.rsqrt(mean_sq + RMS_EPS)
        for cols, out in zip((slice(0, half), slice(half, RET_DV)), outs):
            gate = gate_ref[rows, cols].astype(F32)
            o_ref[rows, cols] = (out * inv_rms * gate).astype(o_ref.dtype)
        return carry

    lax.fori_loop(0, n_chunks, chunk, 0, unroll=True)


def _retention(q, k, v, gate, log_gamma, batch):
    t = q.shape[1]
    s = t // batch
    ts = RET_SEQ_TILE
    tiles_per_seq = s // ts
    hbm = pl.BlockSpec(memory_space=pl.ANY)
    return pl.pallas_call(
        functools.partial(_retention_kernel, n_chunks=ts // RET_CHUNK, batch=batch,
                          tiles_per_seq=tiles_per_seq, ts=ts),
        in_specs=[pl.BlockSpec(memory_space=pltpu.SMEM), hbm, hbm, hbm, hbm],
        out_specs=hbm,
        out_shape=jax.ShapeDtypeStruct((t, RET_HEADS * RET_DV), BF16),
        scratch_shapes=[pltpu.VMEM((RET_DK, RET_DV), F32), pltpu.SMEM((1,), jnp.int32)],
        compiler_params=pltpu.CompilerParams(vmem_limit_bytes=VMEM_LIMIT_BYTES),
        name="retention",
    )(log_gamma, q, k, v, gate)


def _bucket_upper_bounds():
    n = np.arange(0, 4 * REL_MAX_DIST, dtype=np.int64)
    nf = np.maximum(n, 1).astype(np.float32)
    ratio = np.log(nf / np.float32(REL_MAX_EXACT)) / np.float32(
        math.log(REL_MAX_DIST / REL_MAX_EXACT)) * np.float32(REL_BUCKETS - REL_MAX_EXACT)
    large = np.minimum(REL_MAX_EXACT + ratio.astype(np.int32), REL_BUCKETS - 1)
    bucket = np.where(n < REL_MAX_EXACT, n, large)
    assert np.all(np.diff(bucket) >= 0) and bucket[REL_MAX_DIST] == REL_BUCKETS - 1
    return [int(n[bucket == b].max()) for b in range(REL_BUCKETS - 1)]


def _bias_tiles_kernel(tab_ref, o_ref, *, tile, upper):
    h = pl.program_id(0)
    key = lax.broadcasted_iota(jnp.int32, (tile, tile), 0)
    qry = lax.broadcasted_iota(jnp.int32, (tile, tile), 1)
    far = tab_ref[h, REL_BUCKETS - 1]
    for t in (BIAS_DIAGONAL, BIAS_PREVIOUS):
        dist = qry - key + t * tile
        val = jnp.zeros((tile, tile), F32)
        for b in range(REL_BUCKETS - 2, -1, -1):
            val = jnp.where(dist <= upper[b], (tab_ref[h, b] - far) * LOG2_E, val)
        o_ref[t] = jnp.where(dist >= 0, val, NEG_BIG)
    o_ref[BIAS_MASKED] = jnp.full((tile, tile), NEG_BIG, F32)


def _bias_tiles(rel_table_t, tile):
    assert tile >= REL_MAX_DIST
    return pl.pallas_call(
        functools.partial(_bias_tiles_kernel, tile=tile, upper=_bucket_upper_bounds()),
        grid=(DIFF_HEADS,),
        in_specs=[pl.BlockSpec(memory_space=pltpu.SMEM)],
        out_specs=pl.BlockSpec((None, 3, tile, tile), lambda h: (h, 0, 0, 0)),
        out_shape=jax.ShapeDtypeStruct((DIFF_HEADS, 3, tile, tile), F32),
        compiler_params=_params("parallel"),
        name="bias_tiles",
    )(rel_table_t)


def _diff_attn_kernel(qs_ref, k_ref, vt_ref, bias_ref, lam_ref, g_ref, o_ref,
                      s_ref, smax_ref, m_ref, acc_ref, *, tile, lambda_init):
    n_tiles = qs_ref.shape[0]
    assert ATTN_UNROLL % 2 == 0 and NEAR_UNROLL % 2 == 0 and n_tiles % NEAR_UNROLL == 0
    ones_rows = jnp.ones((SUM_ROWS, tile), BF16)
    last = n_tiles - 1

    def key_tile(j):
        return k_ref[pl.ds(pl.multiple_of(j * tile, tile), tile), :]

    def value_rows(j):
        return jnp.concatenate([vt_ref[j], ones_rows], axis=0)

    def put_scores(slot, s):
        s_ref[slot] = s
        smax_ref[slot] = jnp.max(s, axis=0, keepdims=True)

    band = tile // DIAG_PARTS

    def visible(a, r):
        return jnp.concatenate([a[:, r * band:tile], a[:, tile + r * band:]], axis=1)

    def full_width(a, r, fill):
        if r == 0:
            return a
        width = tile - r * band
        pad = jnp.full((a.shape[0], r * band), fill, a.dtype)
        return jnp.concatenate([pad, a[:, :width], pad, a[:, width:]], axis=1)

    def near_scores(i, base):
        i = jnp.minimum(i, last)
        qs = qs_ref[i]
        key_row = pl.multiple_of(i * tile, tile)
        m_diag = None
        for r in range(DIAG_PARTS):
            rows = slice(r * band, (r + 1) * band)
            bias_t = bias_ref[BIAS_DIAGONAL, rows, r * band:]
            s = jnp.dot(k_ref[pl.ds(key_row + r * band, band), :], visible(qs, r),
                        preferred_element_type=F32)
            s = s + jnp.concatenate([bias_t, bias_t], axis=1)
            s_ref[base, rows, :s.shape[1]] = s
            m_band = full_width(jnp.max(s, axis=0, keepdims=True), r, NEG_BIG)
            m_diag = m_band if m_diag is None else jnp.maximum(m_diag, m_band)
        smax_ref[base] = m_diag
        near_rows = tile - REL_MAX_DIST
        bias_t = bias_ref[jnp.where(i >= 1, BIAS_PREVIOUS, BIAS_MASKED), near_rows:]
        s = jnp.dot(key_tile(jnp.maximum(i - 1, 0)), qs, preferred_element_type=F32)
        s = jnp.concatenate([s[:near_rows] + jnp.where(i >= 1, 0.0, NEG_BIG),
                             s[near_rows:] + jnp.concatenate([bias_t, bias_t], axis=1)], axis=0)
        put_scores(base + 1, s)

    def near_reduce(i, base):
        m_diag = smax_ref[base]
        values = value_rows(i)
        acc = None
        for r in range(DIAG_PARTS):
            rows = slice(r * band, (r + 1) * band)
            width = 2 * (tile - r * band)
            p = jnp.exp2(s_ref[base, rows, :width] - visible(m_diag, r)).astype(BF16)
            part = full_width(jnp.dot(values[:, rows], p, preferred_element_type=F32), r, 0.0)
            acc = part if acc is None else acc + part
        m_new = jnp.maximum(m_diag, smax_ref[base + 1])
        p = jnp.exp2(s_ref[base + 1] - m_new).astype(BF16)
        acc = jnp.exp2(m_diag - m_new) * acc + jnp.dot(
            value_rows(jnp.maximum(i - 1, 0)), p, preferred_element_type=F32)
        acc_ref[i] = acc
        m_ref[i] = m_new

    def near_steps(g, carry):
        for u in range(NEAR_UNROLL):
            i = NEAR_UNROLL * g + u
            near_scores(i + 1, 2 * ((u + 1) % 2))
            near_reduce(i, 2 * (u % 2))
        return carry

    near_scores(jnp.int32(0), 0)
    lax.fori_loop(0, n_tiles // NEAR_UNROLL, near_steps, 0)

    def successor(i, j):
        wraps = j == i - 2
        return jnp.where(wraps, i + 1, i), jnp.where(wraps, 0, j + 1)

    def far_scores(i, j, slot):
        i = jnp.minimum(i, last)
        put_scores(slot, jnp.dot(key_tile(j), qs_ref[i], preferred_element_type=F32))

    def far_reduce(i, j, slot):
        m_prev = m_ref[i]
        m_new = jnp.maximum(m_prev, smax_ref[slot])
        p = jnp.exp2(s_ref[slot] - m_new).astype(BF16)
        acc_ref[i] = jnp.exp2(m_prev - m_new) * acc_ref[i] + jnp.dot(
            value_rows(j), p, preferred_element_type=F32)
        m_ref[i] = m_new

    lam = lam_ref[...]
    lam_full = (jnp.exp(jnp.sum(lam[0:1] * lam[1:2], axis=1, keepdims=True))
                - jnp.exp(jnp.sum(lam[2:3] * lam[3:4], axis=1, keepdims=True))
                + lambda_init)

    def emit(i):
        acc = acc_ref[i]
        inv_l = 1.0 / acc[DIFF_DV:DIFF_DV + 1]
        o_t = (acc[:DIFF_DV, :tile] * inv_l[:, :tile]
               - acc[:DIFF_DV, tile:] * (lam_full * inv_l[:, tile:]))
        scale = lax.rsqrt(jnp.mean(o_t * o_t, axis=0, keepdims=True) + RMS_EPS) * (1.0 - lambda_init)
        o_ref[i] = (o_t * scale * g_ref[...]).astype(o_ref.dtype)

    def far_steps(_, carry):
        i, j = carry
        emit(i - 1)
        for u in range(ATTN_UNROLL):
            i_next, j_next = successor(i, j)
            far_scores(i_next, j_next, (u + 1) % 2)
            far_reduce(i, j, u % 2)
            i, j = i_next, j_next
        return i, j

    far_pairs = [(i, j) for i in range(2, n_tiles) for j in range(i - 1)]
    n_loop = len(far_pairs) // ATTN_UNROLL * ATTN_UNROLL
    emitted = {far_pairs[t][0] - 1 for t in range(0, n_loop, ATTN_UNROLL)}
    first = jnp.int32(2), jnp.int32(0)
    far_scores(*first, 0)
    i, j = lax.fori_loop(0, n_loop // ATTN_UNROLL, far_steps, first)
    for u in range(n_loop, len(far_pairs)):
        i_next, j_next = successor(i, j)
        if u + 1 < len(far_pairs):
            far_scores(i_next, j_next, (u + 1) % 2)
        far_reduce(i, j, u % 2)
        i, j = i_next, j_next
    for i in range(n_tiles):
        if i not in emitted:
            emit(i)


def _diff_attention(k, qs_t, v_t, bias, lam, subln, lambda_init):
    b, s, _ = k.shape
    tile = ATTN_TILE
    nk = s // tile
    return pl.pallas_call(
        functools.partial(_diff_attn_kernel, tile=tile, lambda_init=lambda_init),
        grid=(b, DIFF_HEADS),
        in_specs=[pl.BlockSpec((None, nk, 2 * DIFF_DH, 2 * tile), lambda bi, h: (bi, 0, h, 0)),
                  pl.BlockSpec((None, s, 2 * DIFF_DH), lambda bi, h: (bi, 0, h)),
                  pl.BlockSpec((None, nk, DIFF_DV, tile), lambda bi, h: (bi, 0, h, 0)),
                  pl.BlockSpec((None, 3, tile, tile), lambda bi, h: (h, 0, 0, 0)),
                  pl.BlockSpec((4, DIFF_DH), lambda bi, h: (0, 0)),
                  pl.BlockSpec((DIFF_DV, 1), lambda bi, h: (0, 0))],
        out_specs=pl.BlockSpec((None, nk, DIFF_DV, tile), lambda bi, h: (bi, 0, h, 0)),
        out_shape=jax.ShapeDtypeStruct((b, nk, DIFF_HEADS * DIFF_DV, tile), BF16),
        scratch_shapes=[pltpu.VMEM((4, tile, 2 * tile), F32),
                        pltpu.VMEM((4, 1, 2 * tile), F32),
                        pltpu.VMEM((nk, 1, 2 * tile), F32),
                        pltpu.VMEM((nk, DIFF_DV + SUM_ROWS, 2 * tile), F32)],
        compiler_params=_params("parallel", "parallel"),
        name="diff_attention",
    )(qs_t, k, v_t, bias, lam, subln)


def kernel(x, norm_gains, ret_w_in, ret_w_out, diff_w_in, diff_w_out, diff_lambda,
           diff_subln, rel_bias_table, ffn_w_gate, ffn_w_up, ffn_w_down):
    b, s, d = x.shape
    t = b * s
    x2d = x.reshape(t, d)
    gains = norm_gains.reshape(DEPTH, 4, 1, d)

    half = RET_DK // 2
    inv = ROPE_BASE ** (-jnp.arange(half, dtype=F32) / half)
    ang = jnp.arange(s, dtype=F32)[:, None] * inv[None, :]
    cos, sin = jnp.cos(ang), jnp.sin(ang)
    log_gamma = jnp.log1p(-(2.0 ** (-5.0 - jnp.arange(RET_HEADS, dtype=F32))))

    bias = _bias_tiles(rel_bias_table.T, ATTN_TILE)

    assert 2 * DIFF_HEADS * DIFF_DH == DIFF_HEADS * DIFF_DV

    def f32_weights(layer):
        mixer = layer // N_MIXERS
        w_in, w_out = (ret_w_in, ret_w_out) if layer % N_MIXERS == 0 else (diff_w_in, diff_w_out)
        return [(w_in, mixer), (w_out, mixer), (ffn_w_gate, layer), (ffn_w_up, layer),
                (ffn_w_down, layer)]

    weights = [w[index].astype(BF16) for w, index in f32_weights(0)]

    for i in range(DEPTH):
        g = gains[i]
        w_in, w_out, w_gate, w_up, w_down = weights
        if i % N_MIXERS == 0:
            q, k, v, gate = _ret_in_proj(x2d, g[0], w_in, cos, sin)
            y = _retention(q, k, v, gate, log_gamma, b)
        else:
            lambda_init = 0.8 - 0.6 * math.exp(-0.3 * i)
            j = i // N_MIXERS
            k, qs_t, v_t = _diff_in_proj(x2d, g[0], w_in)
            nk = s // ATTN_TILE
            y = _diff_attention(k.reshape(b, s, -1),
                                qs_t.reshape(b, nk, -1, 2 * ATTN_TILE),
                                v_t.reshape(b, nk, -1, ATTN_TILE),
                                bias, diff_lambda[j], diff_subln[j].reshape(DIFF_DV, 1),
                                lambda_init)
        x2d, weights = _post_mixer(y, w_out, x2d, g[1], g[2], w_gate, w_up, w_down, g[3],
                                   to_cast=f32_weights(i + 1) if i + 1 < DEPTH else ())
    return x2d.reshape(b, s, d)
```
